```python
import math
import jax, jax.numpy as jnp
from jax import lax
import numpy as np

D_MODEL = 1024
BATCH = 16
SEQ = 4096
DEPTH = 4
DEC_BATCH = 8
DEC_SEQ = 2048
PAST_LEN = 128

N_MIXERS = 2
N_A_LAYERS = (DEPTH + N_MIXERS - 1) // N_MIXERS
N_B_LAYERS = DEPTH // N_MIXERS

A_HEADS = 8
A_QK_DIM = 64
A_V_DIM = 2 * A_QK_DIM
A_WIDTH = A_HEADS * A_V_DIM
A_QK_WIDTH = A_HEADS * 2 * A_QK_DIM
A_IN = 2 * A_QK_WIDTH + A_WIDTH + A_WIDTH
A_ROT_DIM = A_QK_DIM // 4
ROPE_THETA = 500000.0

B_HEADS = 8
B_KV_HEADS = 2
B_GROUP = B_HEADS // B_KV_HEADS
B_HEAD_DIM = 128
B_WIDTH = B_HEADS * B_HEAD_DIM
B_KV_WIDTH = B_KV_HEADS * B_HEAD_DIM
B_IN = B_WIDTH + 2 * B_KV_WIDTH + B_WIDTH
B_AXIS_DIM = B_HEAD_DIM // 2
AXIAL_THETA = 10000.0

GRID_W = 64
Q_BLOCK = 128
NORM_EPS = 1e-6

kernel_name = "hybrid_diffattn_axialgqa_encoder"


def _rms_norm(x, w):
    x32 = x.astype(jnp.float32)
    y = x32 * lax.rsqrt(jnp.mean(x32 * x32, axis=-1, keepdims=True) + NORM_EPS)
    return (y * w.astype(jnp.float32)).astype(x.dtype)


def _rope_angles(pos, dim, theta):
    inv_freq = theta ** (-(jnp.arange(0, dim, 2, dtype=jnp.float32) / dim))
    ang = pos[:, None] * inv_freq[None, :]
    return jnp.cos(ang), jnp.sin(ang)


def _rotate_half(x, cos, sin):
    x32 = x.astype(jnp.float32)
    half = x32.shape[-1] // 2
    x1, x2 = x32[..., :half], x32[..., half:]
    out = jnp.concatenate([x1 * cos - x2 * sin, x2 * cos + x1 * sin], axis=-1)
    return out.astype(x.dtype)


def _sweep_query_blocks(q, block_fn):
    B, S = q.shape[0], q.shape[1]
    nb = S // Q_BLOCK
    qb = jnp.moveaxis(q.reshape((B, nb, Q_BLOCK) + q.shape[2:]), 1, 0)
    out = lax.map(block_fn, qb)
    out = jnp.moveaxis(out, 0, 1)
    return out.reshape((B, S) + out.shape[3:])


def _diff_attention_mixer(h, w_in, w_out, lam, subln, layer_idx):
    B, S, _ = h.shape
    proj = h @ w_in
    q, k, v, gate = jnp.split(proj, [A_QK_WIDTH, 2 * A_QK_WIDTH, 2 * A_QK_WIDTH + A_WIDTH], axis=-1)
    q = q.reshape(B, S, A_HEADS, 2, A_QK_DIM)
    k = k.reshape(B, S, A_HEADS, 2, A_QK_DIM)
    v = v.reshape(B, S, A_HEADS, A_V_DIM)
    cos, sin = _rope_angles(jnp.arange(S, dtype=jnp.float32), A_ROT_DIM, ROPE_THETA)
    cos, sin = cos[None, :, None, None, :], sin[None, :, None, None, :]
    q = jnp.concatenate([_rotate_half(q[..., :A_ROT_DIM], cos, sin), q[..., A_ROT_DIM:]], axis=-1)
    k = jnp.concatenate([_rotate_half(k[..., :A_ROT_DIM], cos, sin), k[..., A_ROT_DIM:]], axis=-1)
    lam_init = 0.8 - 0.6 * math.exp(-0.3 * layer_idx)
    lam32 = lam.astype(jnp.float32)
    lam_full = (jnp.exp(jnp.sum(lam32[0] * lam32[1])) - jnp.exp(jnp.sum(lam32[2] * lam32[3]))
                + lam_init)
    scale = A_QK_DIM ** -0.5

    def block(qb):
        s = jnp.einsum('bqhmd,bkhmd->bhmqk', qb, k, preferred_element_type=jnp.float32) * scale
        p = jax.nn.softmax(s, axis=-1)
        p_diff = p[:, :, 0] - lam_full * p[:, :, 1]
        o = jnp.einsum('bhqk,bkhe->bqhe', p_diff.astype(v.dtype), v,
                       preferred_element_type=jnp.float32)
        return o.astype(h.dtype)

    o = _sweep_query_blocks(q, block)
    o = _rms_norm(o, subln) * (1.0 - lam_init)
    o = o.reshape(B, S, A_WIDTH) * jax.nn.silu(gate)
    return o @ w_out


def _axial_gqa_mixer(h, w_in, w_out, q_norm, k_norm):
    B, S, _ = h.shape
    proj = h @ w_in
    q, k, v, gate = jnp.split(proj, [B_WIDTH, B_WIDTH + B_KV_WIDTH, B_WIDTH + 2 * B_KV_WIDTH], axis=-1)
    q = _rms_norm(q.reshape(B, S, B_HEADS, B_HEAD_DIM), q_norm)
    k = _rms_norm(k.reshape(B, S, B_KV_HEADS, B_HEAD_DIM), k_norm)
    v = v.reshape(B, S, B_KV_HEADS, B_HEAD_DIM)
    rows = S // GRID_W
    row_ids = jnp.repeat(jnp.arange(rows, dtype=jnp.float32), GRID_W)
    col_ids = jnp.tile(jnp.arange(GRID_W, dtype=jnp.float32), rows)
    cr, sr = _rope_angles(row_ids, B_AXIS_DIM, AXIAL_THETA)
    cc, sc = _rope_angles(col_ids, B_AXIS_DIM, AXIAL_THETA)
    cr, sr, cc, sc = (a[None, :, None, :] for a in (cr, sr, cc, sc))

    def axial(t):
        return jnp.concatenate([_rotate_half(t[..., :B_AXIS_DIM], cr, sr),
                                _rotate_half(t[..., B_AXIS_DIM:], cc, sc)], axis=-1)

    q = axial(q).reshape(B, S, B_KV_HEADS, B_GROUP, B_HEAD_DIM)
    k = axial(k)
    scale = B_HEAD_DIM ** -0.5

    def block(qb):
        s = jnp.einsum('bqkgd,bskd->bkgqs', qb, k, preferred_element_type=jnp.float32) * scale
        p = jax.nn.softmax(s, axis=-1)
        o = jnp.einsum('bkgqs,bskd->bqkgd', p.astype(v.dtype), v,
                       preferred_element_type=jnp.float32)
        return o.astype(h.dtype)

    o = _sweep_query_blocks(q, block)
    o = o.reshape(B, S, B_WIDTH) * jax.nn.silu(gate)
    return o @ w_out


def _trunk(x, norm_pre, norm_post, a_w_in, a_w_out, a_lam, a_subln,
           b_w_in, b_w_out, b_q_norm, b_k_norm):
    for i in range(DEPTH):
        h = _rms_norm(x, norm_pre[i])
        j = i // N_MIXERS
        if i % N_MIXERS == 0:
            m = _diff_attention_mixer(h, a_w_in[j], a_w_out[j], a_lam[j], a_subln[j], i)
        else:
            m = _axial_gqa_mixer(h, b_w_in[j], b_w_out[j], b_q_norm[j], b_k_norm[j])
        x = x + _rms_norm(m, norm_post[i])
    return x


def setup_inputs(seed: int = 0) -> dict:
    key = jax.random.key(seed)
    ks = jax.random.split(key, 12)
    f32 = jnp.float32
    return {
        "x_prompt": jax.random.normal(ks[0], (BATCH, SEQ, D_MODEL), f32),
        "x_sample": jax.random.normal(ks[1], (DEC_BATCH, DEC_SEQ, D_MODEL), f32),
        "norm_pre": 1.0 + 0.02 * jax.random.normal(ks[2], (DEPTH, D_MODEL), f32),
        "norm_post": 1.0 + 0.02 * jax.random.normal(ks[3], (DEPTH, D_MODEL), f32),
        "a_w_in": jax.random.normal(ks[4], (N_A_LAYERS, D_MODEL, A_IN), f32) * D_MODEL ** -0.5,
        "a_w_out": jax.random.normal(ks[5], (N_A_LAYERS, A_WIDTH, D_MODEL), f32) * A_WIDTH ** -0.5,
        "a_lam": 0.1 * jax.random.normal(ks[6], (N_A_LAYERS, 4, A_QK_DIM), f32),
        "a_subln": 1.0 + 0.02 * jax.random.normal(ks[7], (N_A_LAYERS, A_V_DIM), f32),
        "b_w_in": jax.random.normal(ks[8], (N_B_LAYERS, D_MODEL, B_IN), f32) * D_MODEL ** -0.5,
        "b_w_out": jax.random.normal(ks[9], (N_B_LAYERS, B_WIDTH, D_MODEL), f32) * B_WIDTH ** -0.5,
        "b_q_norm": 1.0 + 0.02 * jax.random.normal(ks[10], (N_B_LAYERS, B_HEAD_DIM), f32),
        "b_k_norm": 1.0 + 0.02 * jax.random.normal(ks[11], (N_B_LAYERS, B_HEAD_DIM), f32),
    }


def reference(x_prompt, x_sample, norm_pre, norm_post, a_w_in, a_w_out, a_lam, a_subln,
              b_w_in, b_w_out, b_q_norm, b_k_norm):
    y_prompt = _trunk(x_prompt, norm_pre, norm_post, a_w_in, a_w_out, a_lam, a_subln,
                      b_w_in, b_w_out, b_q_norm, b_k_norm)
    y_sample = _trunk(x_sample, norm_pre, norm_post, a_w_in, a_w_out, a_lam, a_subln,
                      b_w_in, b_w_out, b_q_norm, b_k_norm)
    return (y_prompt, y_sample)
```

```python
import functools
import math

import jax
import jax.numpy as jnp
from jax import lax
from jax.experimental import pallas as pl
from jax.experimental.pallas import tpu as pltpu

D_MODEL = 1024
DEPTH = 4
NORM_EPS = 1e-6

A_HEADS = 8
A_QK_DIM = 64
A_ROT_DIM = 16
A_IN = 4096
ROPE_THETA = 500000.0

B_HEADS = 8
B_KV_HEADS = 2
B_GROUP = B_HEADS // B_KV_HEADS
B_HEAD_DIM = 128
B_IN = 2560
B_AXIS_DIM = 64
AXIAL_THETA = 10000.0
GRID_W = 64

LANES = 128
LOG2E = 1.4426950408889634
VMEM_LIMIT_BYTES = 56 * 1024 * 1024

PROJ_ROWS = 256
PROJ_COLS = 512
ATTN_ROWS = 512
KEY_CHUNK = 512


def _compiler_params(n_axes):
    return pltpu.CompilerParams(
        dimension_semantics=("parallel",) * n_axes,
        vmem_limit_bytes=VMEM_LIMIT_BYTES)


def _angles(pos, dim, theta):
    inv_freq = theta ** (-(jnp.arange(0, dim, 2, dtype=jnp.float32) / dim))
    return pos[:, None] * inv_freq[None, :]


def _rope_tables_a(seq):
    ang = _angles(jnp.arange(seq, dtype=jnp.float32), A_ROT_DIM, ROPE_THETA)
    cos, sin = jnp.cos(ang), jnp.sin(ang)
    half = A_ROT_DIM // 2
    one = jnp.ones((seq, A_QK_DIM - A_ROT_DIM), jnp.float32)
    zero_h = jnp.zeros((seq, half), jnp.float32)
    zero_r = jnp.zeros((seq, A_QK_DIM - A_ROT_DIM), jnp.float32)
    c = jnp.concatenate([cos, cos, one], axis=1)
    sa = jnp.concatenate([-sin, zero_h, zero_r], axis=1)
    sb = jnp.concatenate([zero_h, sin, zero_r], axis=1)
    tile2 = lambda t: jnp.concatenate([t, t], axis=1)
    return tile2(c), tile2(sa), tile2(sb)


def _rope_tables_b(seq):
    rows = seq // GRID_W
    row_ids = jnp.repeat(jnp.arange(rows, dtype=jnp.float32), GRID_W)
    col_ids = jnp.tile(jnp.arange(GRID_W, dtype=jnp.float32), rows)
    ar = _angles(row_ids, B_AXIS_DIM, AXIAL_THETA)
    ac = _angles(col_ids, B_AXIS_DIM, AXIAL_THETA)
    z = jnp.zeros_like(ar)
    c = jnp.concatenate([jnp.cos(ar), jnp.cos(ar), jnp.cos(ac), jnp.cos(ac)], axis=1)
    sa = jnp.concatenate([-jnp.sin(ar), z, -jnp.sin(ac), z], axis=1)
    sb = jnp.concatenate([z, jnp.sin(ar), z, jnp.sin(ac)], axis=1)
    return c, sa, sb


def _rotate(y, c, sa, sb, shift):
    return (y * c + pltpu.roll(y, LANES - shift, axis=1) * sa
            + pltpu.roll(y, shift, axis=1) * sb)


def _silu(g):
    return g / (1.0 + jnp.exp(-g))


def _rms_scale(y):
    return lax.rsqrt(jnp.mean(y * y, axis=-1, keepdims=True) + NORM_EPS)


def _inproj_kernel(x_ref, nw_ref, w_ref, c_ref, sa_ref, sb_ref, qn_ref, kn_ref,
                   o_ref, *, mixer):
    x = x_ref[...]
    h = (x * _rms_scale(x) * nw_ref[...]).astype(jnp.bfloat16)
    c, sa, sb = c_ref[...], sa_ref[...], sb_ref[...]
    n_out = o_ref.shape[1]
    if mixer == "a":
        n_q, n_k, shift = A_HEADS, A_HEADS, A_ROT_DIM // 2
        q_scale = (A_QK_DIM ** -0.5) * LOG2E
        n_plain = A_HEADS
    else:
        n_q, n_k, shift = B_HEADS, B_KV_HEADS, B_AXIS_DIM // 2
        q_scale = (B_HEAD_DIM ** -0.5) * LOG2E
        n_plain = B_KV_HEADS
    for col0 in range(0, n_out, PROJ_COLS):
        y = jnp.dot(h, w_ref[:, col0:col0 + PROJ_COLS],
                    preferred_element_type=jnp.float32)
        for t in range(PROJ_COLS // LANES):
            tile = col0 // LANES + t
            yt = y[:, t * LANES:(t + 1) * LANES]
            if tile < n_q + n_k:
                is_q = tile < n_q
                if mixer == "b":
                    yt = yt * _rms_scale(yt) * (qn_ref[...] if is_q else kn_ref[...])
                yt = _rotate(yt, c, sa, sb, shift)
                if is_q:
                    yt = yt * q_scale
            elif tile >= n_q + n_k + n_plain:
                yt = _silu(yt)
            o_ref[:, tile * LANES:(tile + 1) * LANES] = yt.astype(o_ref.dtype)


def _inproj(x2, seq, norm_w, w_in, tables, qn, kn, mixer):
    n_tok = x2.shape[0]
    n_out = w_in.shape[1]
    per_seq = seq // PROJ_ROWS
    tab_spec = pl.BlockSpec((PROJ_ROWS, LANES), lambda i: (i % per_seq, 0))
    vec_spec = pl.BlockSpec((1, LANES), lambda i: (0, 0))
    return pl.pallas_call(
        functools.partial(_inproj_kernel, mixer=mixer),
        grid=(n_tok // PROJ_ROWS,),
        in_specs=[
            pl.BlockSpec((PROJ_ROWS, D_MODEL), lambda i: (i, 0)),
            pl.BlockSpec((1, D_MODEL), lambda i: (0, 0)),
            pl.BlockSpec((D_MODEL, n_out), lambda i: (0, 0)),
            tab_spec, tab_spec, tab_spec, vec_spec, vec_spec,
        ],
        out_specs=pl.BlockSpec((PROJ_ROWS, n_out), lambda i: (i, 0)),
        out_shape=jax.ShapeDtypeStruct((n_tok, n_out), jnp.bfloat16),
        compiler_params=_compiler_params(1),
        name=f"inproj_{mixer}",
    )(x2, norm_w.reshape(1, D_MODEL), w_in, *tables,
      qn.reshape(1, LANES), kn.reshape(1, LANES))


def _lane_fold(x, op):
    out = x[:, 0:LANES]
    for t in range(1, x.shape[1] // LANES):
        out = op(out, x[:, t * LANES:(t + 1) * LANES])
    return out


def _softmax_pv(qs, k_ref, v_ref, s_scr):
    rows = qs.shape[0]
    n_chunks = s_scr.shape[0]

    def scores(ci, m_run):
        k0 = pl.multiple_of(ci * KEY_CHUNK, KEY_CHUNK)
        s = lax.dot_general(qs, k_ref[pl.ds(k0, KEY_CHUNK), :],
                            (((1,), (1,)), ((), ())),
                            preferred_element_type=jnp.float32)
        s_scr[ci] = s
        return jnp.maximum(m_run, _lane_fold(s, jnp.maximum))

    m_run = lax.fori_loop(0, n_chunks, scores,
                          jnp.full((rows, LANES), -jnp.inf, jnp.float32))
    m = jnp.max(m_run, axis=-1, keepdims=True)

    def weigh(ci, carry):
        l_run, acc = carry
        k0 = pl.multiple_of(ci * KEY_CHUNK, KEY_CHUNK)
        p = jnp.exp2(s_scr[ci] - m)
        l_run = l_run + _lane_fold(p, jnp.add)
        acc = acc + jnp.dot(p.astype(jnp.bfloat16), v_ref[pl.ds(k0, KEY_CHUNK), :],
                            preferred_element_type=jnp.float32)
        return l_run, acc

    zeros = jnp.zeros((rows, LANES), jnp.float32)
    l_run, acc = lax.fori_loop(0, n_chunks, weigh, (zeros, zeros))
    return acc, jnp.sum(l_run, axis=-1, keepdims=True)


def _attn_a_kernel(lam_ref, subln_ref, q_ref, k_ref, v_ref, g_ref, o_ref, s_scr,
                   *, lam_init):
    bq = q_ref.shape[0]
    q = q_ref[...]
    lane = lax.broadcasted_iota(jnp.int32, q.shape, 1)
    zero = jnp.zeros_like(q)
    qs = jnp.concatenate([jnp.where(lane < A_QK_DIM, q, zero),
                          jnp.where(lane >= A_QK_DIM, q, zero)], axis=0)
    acc, l = _softmax_pv(qs, k_ref, v_ref, s_scr)
    lam = lam_ref[...]
    lam_full = (jnp.exp(jnp.sum(lam[0:1] * lam[1:2], axis=-1, keepdims=True))
                - jnp.exp(jnp.sum(lam[2:3] * lam[3:4], axis=-1, keepdims=True))
                + lam_init)
    o = acc[:bq] / l[:bq] - lam_full * (acc[bq:] / l[bq:])
    o = o * _rms_scale(o) * subln_ref[...] * (1.0 - lam_init)
    o_ref[...] = (o * g_ref[...].astype(jnp.float32)).astype(o_ref.dtype)


def _attn_a(proj, batch, seq, lam, subln, lam_init):
    bq = min(ATTN_ROWS // 2, seq)
    nq = seq // bq
    n_chunks = seq // KEY_CHUNK
    return pl.pallas_call(
        functools.partial(_attn_a_kernel, lam_init=lam_init),
        grid=(batch, A_HEADS, nq),
        in_specs=[
            pl.BlockSpec((4, A_QK_DIM), lambda b, h, i: (0, 0)),
            pl.BlockSpec((1, LANES), lambda b, h, i: (0, 0)),
            pl.BlockSpec((bq, LANES), lambda b, h, i: (b * nq + i, h)),
            pl.BlockSpec((seq, LANES), lambda b, h, i: (b, A_HEADS + h)),
            pl.BlockSpec((seq, LANES), lambda b, h, i: (b, 2 * A_HEADS + h)),
            pl.BlockSpec((bq, LANES), lambda b, h, i: (b * nq + i, 3 * A_HEADS + h)),
        ],
        out_specs=pl.BlockSpec((bq, LANES), lambda b, h, i: (b * nq + i, h)),
        out_shape=jax.ShapeDtypeStruct((batch * seq, A_HEADS * LANES), jnp.bfloat16),
        scratch_shapes=[pltpu.VMEM((n_chunks, 2 * bq, KEY_CHUNK), jnp.float32)],
        compiler_params=_compiler_params(3),
        name="attn_a",
    )(lam, subln.reshape(1, LANES), proj, proj, proj, proj)


def _attn_b_kernel(q_ref, k_ref, v_ref, g_ref, o_ref, s_scr):
    bq = q_ref.shape[0]
    qs = jnp.concatenate(
        [q_ref[:, g * LANES:(g + 1) * LANES] for g in range(B_GROUP)], axis=0)
    acc, l = _softmax_pv(qs, k_ref, v_ref, s_scr)
    o = acc / l
    for g in range(B_GROUP):
        gate = g_ref[:, g * LANES:(g + 1) * LANES].astype(jnp.float32)
        o_ref[:, g * LANES:(g + 1) * LANES] = (
            o[g * bq:(g + 1) * bq] * gate).astype(o_ref.dtype)


def _attn_b(proj, batch, seq):
    bq = min(ATTN_ROWS // B_GROUP, seq)
    nq = seq // bq
    n_chunks = seq // KEY_CHUNK
    group_cols = B_GROUP * LANES
    k_tile0 = B_HEADS
    v_tile0 = B_HEADS + B_KV_HEADS
    g_blk0 = (B_HEADS + 2 * B_KV_HEADS) * LANES // group_cols
    return pl.pallas_call(
        _attn_b_kernel,
        grid=(batch, B_KV_HEADS, nq),
        in_specs=[
            pl.BlockSpec((bq, group_cols), lambda b, kv, i: (b * nq + i, kv)),
            pl.BlockSpec((seq, LANES), lambda b, kv, i: (b, k_tile0 + kv)),
            pl.BlockSpec((seq, LANES), lambda b, kv, i: (b, v_tile0 + kv)),
            pl.BlockSpec((bq, group_cols), lambda b, kv, i: (b * nq + i, g_blk0 + kv)),
        ],
        out_specs=pl.BlockSpec((bq, group_cols), lambda b, kv, i: (b * nq + i, kv)),
        out_shape=jax.ShapeDtypeStruct((batch * seq, B_HEADS * LANES), jnp.bfloat16),
        scratch_shapes=[pltpu.VMEM((n_chunks, B_GROUP * bq, KEY_CHUNK), jnp.float32)],
        compiler_params=_compiler_params(3),
        name="attn_b",
    )(proj, proj, proj, proj)


def _outproj_kernel(o_ref, w_ref, nw_ref, x_ref, y_ref):
    m = jnp.dot(o_ref[...], w_ref[...], preferred_element_type=jnp.float32)
    y_ref[...] = x_ref[...] + m * _rms_scale(m) * nw_ref[...]


def _outproj(o, w_out, norm_w, x2):
    n_tok = x2.shape[0]
    row_spec = pl.BlockSpec((PROJ_ROWS, D_MODEL), lambda i: (i, 0))
    return pl.pallas_call(
        _outproj_kernel,
        grid=(n_tok // PROJ_ROWS,),
        in_specs=[
            row_spec,
            pl.BlockSpec((D_MODEL, D_MODEL), lambda i: (0, 0)),
            pl.BlockSpec((1, D_MODEL), lambda i: (0, 0)),
            row_spec,
        ],
        out_specs=row_spec,
        out_shape=jax.ShapeDtypeStruct((n_tok, D_MODEL), jnp.float32),
        compiler_params=_compiler_params(1),
        name="outproj",
    )(o, w_out, norm_w.reshape(1, D_MODEL), x2)


def _trunk(x, norm_pre, norm_post, a_w_in, a_w_out, a_lam, a_subln,
           b_w_in, b_w_out, b_q_norm, b_k_norm):
    batch, seq, _ = x.shape
    x2 = x.reshape(batch * seq, D_MODEL)
    tables_a = _rope_tables_a(seq)
    tables_b = _rope_tables_b(seq)
    ones = jnp.ones((LANES,), jnp.float32)
    for i in range(DEPTH):
        j = i // 2
        if i % 2 == 0:
            proj = _inproj(x2, seq, norm_pre[i], a_w_in[j], tables_a, ones, ones, "a")
            lam_init = 0.8 - 0.6 * math.exp(-0.3 * i)
            o = _attn_a(proj, batch, seq, a_lam[j], a_subln[j], lam_init)
            x2 = _outproj(o, a_w_out[j], norm_post[i], x2)
        else:
            proj = _inproj(x2, seq, norm_pre[i], b_w_in[j], tables_b,
                           b_q_norm[j], b_k_norm[j], "b")
            o = _attn_b(proj, batch, seq)
            x2 = _outproj(o, b_w_out[j], norm_post[i], x2)
    return x2.reshape(batch, seq, D_MODEL)


def kernel(x_prompt, x_sample, norm_pre, norm_post, a_w_in, a_w_out, a_lam, a_subln,
           b_w_in, b_w_out, b_q_norm, b_k_norm):
    bf16 = jnp.bfloat16
    weights = (norm_pre, norm_post, a_w_in.astype(bf16), a_w_out.astype(bf16), a_lam,
               a_subln, b_w_in.astype(bf16), b_w_out.astype(bf16), b_q_norm, b_k_norm)
    return (_trunk(x_prompt, *weights), _trunk(x_sample, *weights))
```

```python
import functools
import math

import jax
import jax.numpy as jnp
from jax import lax
from jax.experimental import pallas as pl
from jax.experimental.pallas import tpu as pltpu

D_MODEL = 1024
DEPTH = 4
NORM_EPS = 1e-6

A_HEADS = 8
A_QK_DIM = 64
A_ROT_DIM = 16
A_IN = 4096
ROPE_THETA = 500000.0

B_HEADS = 8
B_KV_HEADS = 2
B_GROUP = B_HEADS // B_KV_HEADS
B_HEAD_DIM = 128
B_IN = 2560
B_AXIS_DIM = 64
AXIAL_THETA = 10000.0
GRID_W = 64

LANES = 128
LOG2E = 1.4426950408889634
VMEM_LIMIT_BYTES = 56 * 1024 * 1024

PROJ_ROWS = 256
PROJ_COLS = 512
ATTN_ROWS = 512
KEY_CHUNK = 512
B_Q_SPLIT = 2


def _compiler_params(n_axes):
    return pltpu.CompilerParams(
        dimension_semantics=("parallel",) * n_axes,
        vmem_limit_bytes=VMEM_LIMIT_BYTES)


def _angles(pos, dim, theta):
    inv_freq = theta ** (-(jnp.arange(0, dim, 2, dtype=jnp.float32) / dim))
    return pos[:, None] * inv_freq[None, :]


def _rope_tables_a(seq):
    ang = _angles(jnp.arange(seq, dtype=jnp.float32), A_ROT_DIM, ROPE_THETA)
    cos, sin = jnp.cos(ang), jnp.sin(ang)
    half = A_ROT_DIM // 2
    one = jnp.ones((seq, A_QK_DIM - A_ROT_DIM), jnp.float32)
    zero_h = jnp.zeros((seq, half), jnp.float32)
    zero_r = jnp.zeros((seq, A_QK_DIM - A_ROT_DIM), jnp.float32)
    c = jnp.concatenate([cos, cos, one], axis=1)
    sa = jnp.concatenate([-sin, zero_h, zero_r], axis=1)
    sb = jnp.concatenate([zero_h, sin, zero_r], axis=1)
    tile2 = lambda t: jnp.concatenate([t, t], axis=1)
    return tile2(c), tile2(sa), tile2(sb)


def _rope_tables_b(seq):
    rows = seq // GRID_W
    row_ids = jnp.repeat(jnp.arange(rows, dtype=jnp.float32), GRID_W)
    col_ids = jnp.tile(jnp.arange(GRID_W, dtype=jnp.float32), rows)
    ar = _angles(row_ids, B_AXIS_DIM, AXIAL_THETA)
    ac = _angles(col_ids, B_AXIS_DIM, AXIAL_THETA)
    z = jnp.zeros_like(ar)
    c = jnp.concatenate([jnp.cos(ar), jnp.cos(ar), jnp.cos(ac), jnp.cos(ac)], axis=1)
    sa = jnp.concatenate([-jnp.sin(ar), z, -jnp.sin(ac), z], axis=1)
    sb = jnp.concatenate([z, jnp.sin(ar), z, jnp.sin(ac)], axis=1)
    return c, sa, sb


def _rotate(y, c, sa, sb, shift):
    return (y * c + pltpu.roll(y, LANES - shift, axis=1) * sa
            + pltpu.roll(y, shift, axis=1) * sb)


def _silu(g):
    return g / (1.0 + jnp.exp(-g))


def _rms_scale(y):
    return lax.rsqrt(jnp.mean(y * y, axis=-1, keepdims=True) + NORM_EPS)


def _inproj_kernel(x_ref, nw_ref, w_ref, c_ref, sa_ref, sb_ref, qn_ref, kn_ref,
                   o_ref, *, mixer):
    x = x_ref[...]
    h = (x * _rms_scale(x) * nw_ref[...]).astype(jnp.bfloat16)
    c, sa, sb = c_ref[...], sa_ref[...], sb_ref[...]
    n_out = o_ref.shape[1]
    if mixer == "a":
        n_q, n_k, shift = A_HEADS, A_HEADS, A_ROT_DIM // 2
        q_scale = (A_QK_DIM ** -0.5) * LOG2E
        n_plain = A_HEADS
    else:
        n_q, n_k, shift = B_HEADS, B_KV_HEADS, B_AXIS_DIM // 2
        q_scale = (B_HEAD_DIM ** -0.5) * LOG2E
        n_plain = B_KV_HEADS
    for col0 in range(0, n_out, PROJ_COLS):
        y = jnp.dot(h, w_ref[:, col0:col0 + PROJ_COLS],
                    preferred_element_type=jnp.float32)
        for t in range(PROJ_COLS // LANES):
            tile = col0 // LANES + t
            yt = y[:, t * LANES:(t + 1) * LANES]
            if tile < n_q + n_k:
                is_q = tile < n_q
                if mixer == "b":
                    yt = yt * _rms_scale(yt) * (qn_ref[...] if is_q else kn_ref[...])
                yt = _rotate(yt, c, sa, sb, shift)
                if is_q:
                    yt = yt * q_scale
            elif tile >= n_q + n_k + n_plain:
                yt = _silu(yt)
            o_ref[:, tile * LANES:(tile + 1) * LANES] = yt.astype(o_ref.dtype)


def _inproj(x2, seq, norm_w, w_in, tables, qn, kn, mixer):
    n_tok = x2.shape[0]
    n_out = w_in.shape[1]
    per_seq = seq // PROJ_ROWS
    tab_spec = pl.BlockSpec((PROJ_ROWS, LANES), lambda i: (i % per_seq, 0))
    vec_spec = pl.BlockSpec((1, LANES), lambda i: (0, 0))
    return pl.pallas_call(
        functools.partial(_inproj_kernel, mixer=mixer),
        grid=(n_tok // PROJ_ROWS,),
        in_specs=[
            pl.BlockSpec((PROJ_ROWS, D_MODEL), lambda i: (i, 0)),
            pl.BlockSpec((1, D_MODEL), lambda i: (0, 0)),
            pl.BlockSpec((D_MODEL, n_out), lambda i: (0, 0)),
            tab_spec, tab_spec, tab_spec, vec_spec, vec_spec,
        ],
        out_specs=pl.BlockSpec((PROJ_ROWS, n_out), lambda i: (i, 0)),
        out_shape=jax.ShapeDtypeStruct((n_tok, n_out), jnp.bfloat16),
        compiler_params=_compiler_params(1),
        name=f"inproj_{mixer}",
    )(x2, norm_w.reshape(1, D_MODEL), w_in, *tables,
      qn.reshape(1, LANES), kn.reshape(1, LANES))


def _lane_fold(x, op):
    out = x[:, 0:LANES]
    for t in range(1, x.shape[1] // LANES):
        out = op(out, x[:, t * LANES:(t + 1) * LANES])
    return out


def _attention_groups(make_qs, finish, n_groups, k_ref, v_ref, vaug_scr, bufs):
    n_chunks, rows, _ = bufs[0][0].shape
    vaug_scr[:, 0:LANES] = v_ref[...]
    vaug_scr[:, LANES:2 * LANES] = jnp.ones((v_ref.shape[0], LANES), vaug_scr.dtype)

    def score_chunk(qs, c, s_scr, m_run):
        s = lax.dot_general(qs, k_ref[c * KEY_CHUNK:(c + 1) * KEY_CHUNK, :],
                            (((1,), (1,)), ((), ())),
                            preferred_element_type=jnp.float32)
        s_scr[c] = s
        m_c = _lane_fold(s, jnp.maximum)
        return m_c if m_run is None else jnp.maximum(m_run, m_c)

    def weigh_chunk(c, s_scr, m_cur, acc):
        s = s_scr[c]
        p = jnp.concatenate(
            [jnp.exp2(s[:, t * LANES:(t + 1) * LANES] - m_cur)
             for t in range(KEY_CHUNK // LANES)], axis=1).astype(jnp.bfloat16)
        pv = jnp.dot(p, vaug_scr[c * KEY_CHUNK:(c + 1) * KEY_CHUNK, :],
                     preferred_element_type=jnp.float32)
        return pv if acc is None else acc + pv

    def stage(g_cur, g_next, parity):
        s_cur, m_ref = bufs[parity]
        s_next, m_next = bufs[1 - parity]
        if g_next is not None:
            qs_next = make_qs(g_next)
        if g_cur is not None:
            m_cur = m_ref[...]
        m_run, acc = None, None
        for c in range(n_chunks):
            if g_next is not None:
                m_run = score_chunk(qs_next, c, s_next, m_run)
            if g_cur is not None:
                acc = weigh_chunk(c, s_cur, m_cur, acc)
        if g_next is not None:
            m = jnp.max(m_run, axis=-1, keepdims=True)
            m_next[...] = jnp.broadcast_to(m, (rows, LANES))
        if g_cur is not None:
            finish(g_cur, acc)

    stage(None, 0, 1)

    def pair(j, carry):
        g = 2 * j
        stage(g, g + 1, 0)
        stage(g + 1, g + 2, 1)
        return carry

    lax.fori_loop(0, n_groups // 2 - 1, pair, 0)
    stage(n_groups - 2, n_groups - 1, 0)
    stage(n_groups - 1, None, 1)


def _row_start(g, bq):
    return g * bq if isinstance(g, int) else pl.multiple_of(g * bq, bq)


def _attn_a_kernel(lam_ref, subln_ref, q_ref, k_ref, v_ref, g_ref, o_ref,
                   vaug_scr, s0_scr, m0_scr, s1_scr, m1_scr, *, lam_init):
    bq = ATTN_ROWS // 2
    n_groups = q_ref.shape[0] // bq
    lam = lam_ref[...]
    lam_full = (jnp.exp(jnp.sum(lam[0:1] * lam[1:2], axis=-1, keepdims=True))
                - jnp.exp(jnp.sum(lam[2:3] * lam[3:4], axis=-1, keepdims=True))
                + lam_init)
    subln = subln_ref[...] * (1.0 - lam_init)

    def make_qs(g):
        q = q_ref[pl.ds(_row_start(g, bq), bq), :]
        lane = lax.broadcasted_iota(jnp.int32, q.shape, 1)
        zero = jnp.zeros_like(q)
        return jnp.concatenate([jnp.where(lane < A_QK_DIM, q, zero),
                                jnp.where(lane >= A_QK_DIM, q, zero)], axis=0)

    def finish(g, acc):
        r0 = _row_start(g, bq)
        o = acc[:, 0:LANES] / acc[:, LANES:2 * LANES]
        o = o[:bq] - lam_full * o[bq:]
        o = o * _rms_scale(o) * subln
        gate = g_ref[pl.ds(r0, bq), :].astype(jnp.float32)
        o_ref[pl.ds(r0, bq), :] = (o * gate).astype(o_ref.dtype)

    _attention_groups(make_qs, finish, n_groups, k_ref, v_ref, vaug_scr,
                      ((s0_scr, m0_scr), (s1_scr, m1_scr)))


def _attn_scratch(seq):
    n_chunks = seq // KEY_CHUNK
    scores = pltpu.VMEM((n_chunks, ATTN_ROWS, KEY_CHUNK), jnp.float32)
    maxima = pltpu.VMEM((ATTN_ROWS, LANES), jnp.float32)
    return [pltpu.VMEM((seq, 2 * LANES), jnp.bfloat16), scores, maxima, scores, maxima]


def _attn_a(proj, batch, seq, lam, subln, lam_init):
    head_spec = lambda tile0: pl.BlockSpec((seq, LANES), lambda b, h: (b, tile0 + h))
    return pl.pallas_call(
        functools.partial(_attn_a_kernel, lam_init=lam_init),
        grid=(batch, A_HEADS),
        in_specs=[
            pl.BlockSpec((4, A_QK_DIM), lambda b, h: (0, 0)),
            pl.BlockSpec((1, LANES), lambda b, h: (0, 0)),
            head_spec(0), head_spec(A_HEADS), head_spec(2 * A_HEADS), head_spec(3 * A_HEADS),
        ],
        out_specs=head_spec(0),
        out_shape=jax.ShapeDtypeStruct((batch * seq, A_HEADS * LANES), jnp.bfloat16),
        scratch_shapes=_attn_scratch(seq),
        compiler_params=_compiler_params(2),
        name="attn_a",
    )(lam, subln.reshape(1, LANES), proj, proj, proj, proj)


def _attn_b_kernel(q_ref, k_ref, v_ref, g_ref, o_ref,
                   vaug_scr, s0_scr, m0_scr, s1_scr, m1_scr):
    bq = ATTN_ROWS // B_GROUP
    n_groups = q_ref.shape[0] // bq

    def make_qs(g):
        r0 = _row_start(g, bq)
        return jnp.concatenate(
            [q_ref[pl.ds(r0, bq), i * LANES:(i + 1) * LANES] for i in range(B_GROUP)], axis=0)

    def finish(g, acc):
        r0 = _row_start(g, bq)
        o = acc[:, 0:LANES] / acc[:, LANES:2 * LANES]
        for i in range(B_GROUP):
            gate = g_ref[pl.ds(r0, bq), i * LANES:(i + 1) * LANES].astype(jnp.float32)
            o_ref[pl.ds(r0, bq), i * LANES:(i + 1) * LANES] = (
                o[i * bq:(i + 1) * bq] * gate).astype(o_ref.dtype)

    _attention_groups(make_qs, finish, n_groups, k_ref, v_ref, vaug_scr,
                      ((s0_scr, m0_scr), (s1_scr, m1_scr)))


def _attn_b(proj, batch, seq):
    q_rows = seq // B_Q_SPLIT
    group_cols = B_GROUP * LANES
    k_tile0 = B_HEADS
    v_tile0 = B_HEADS + B_KV_HEADS
    g_blk0 = (B_HEADS + 2 * B_KV_HEADS) * LANES // group_cols
    q_spec = lambda blk0: pl.BlockSpec(
        (q_rows, group_cols), lambda b, kv, i: (b * B_Q_SPLIT + i, blk0 + kv))
    kv_spec = lambda tile0: pl.BlockSpec((seq, LANES), lambda b, kv, i: (b, tile0 + kv))
    return pl.pallas_call(
        _attn_b_kernel,
        grid=(batch, B_KV_HEADS, B_Q_SPLIT),
        in_specs=[q_spec(0), kv_spec(k_tile0), kv_spec(v_tile0), q_spec(g_blk0)],
        out_specs=q_spec(0),
        out_shape=jax.ShapeDtypeStruct((batch * seq, B_HEADS * LANES), jnp.bfloat16),
        scratch_shapes=_attn_scratch(seq),
        compiler_params=_compiler_params(3),
        name="attn_b",
    )(proj, proj, proj, proj)


def _outproj_kernel(o_ref, w_ref, nw_ref, x_ref, y_ref):
    m = jnp.dot(o_ref[...], w_ref[...], preferred_element_type=jnp.float32)
    y_ref[...] = x_ref[...] + m * _rms_scale(m) * nw_ref[...]


def _outproj(o, w_out, norm_w, x2):
    n_tok = x2.shape[0]
    row_spec = pl.BlockSpec((PROJ_ROWS, D_MODEL), lambda i: (i, 0))
    return pl.pallas_call(
        _outproj_kernel,
        grid=(n_tok // PROJ_ROWS,),
        in_specs=[
            row_spec,
            pl.BlockSpec((D_MODEL, D_MODEL), lambda i: (0, 0)),
            pl.BlockSpec((1, D_MODEL), lambda i: (0, 0)),
            row_spec,
        ],
        out_specs=row_spec,
        out_shape=jax.ShapeDtypeStruct((n_tok, D_MODEL), jnp.float32),
        compiler_params=_compiler_params(1),
        name="outproj",
    )(o, w_out, norm_w.reshape(1, D_MODEL), x2)


def _trunk(x, norm_pre, norm_post, a_w_in, a_w_out, a_lam, a_subln,
           b_w_in, b_w_out, b_q_norm, b_k_norm):
    batch, seq, _ = x.shape
    x2 = x.reshape(batch * seq, D_MODEL)
    tables_a = _rope_tables_a(seq)
    tables_b = _rope_tables_b(seq)
    ones = jnp.ones((LANES,), jnp.float32)
    for i in range(DEPTH):
        j = i // 2
        if i % 2 == 0:
            proj = _inproj(x2, seq, norm_pre[i], a_w_in[j], tables_a, ones, ones, "a")
            lam_init = 0.8 - 0.6 * math.exp(-0.3 * i)
            o = _attn_a(proj, batch, seq, a_lam[j], a_subln[j], lam_init)
            x2 = _outproj(o, a_w_out[j], norm_post[i], x2)
        else:
            proj = _inproj(x2, seq, norm_pre[i], b_w_in[j], tables_b,
                           b_q_norm[j], b_k_norm[j], "b")
            o = _attn_b(proj, batch, seq)
            x2 = _outproj(o, b_w_out[j], norm_post[i], x2)
    return x2.reshape(batch, seq, D_MODEL)


def kernel(x_prompt, x_sample, norm_pre, norm_post, a_w_in, a_w_out, a_lam, a_subln,
           b_w_in, b_w_out, b_q_norm, b_k_norm):
    bf16 = jnp.bfloat16
    weights = (norm_pre, norm_post, a_w_in.astype(bf16), a_w_out.astype(bf16), a_lam,
               a_subln, b_w_in.astype(bf16), b_w_out.astype(bf16), b_q_norm, b_k_norm)
    return (_trunk(x_prompt, *weights), _trunk(x_sample, *weights))
```

```python
import functools
import math

import jax
import jax.numpy as jnp
from jax import lax
from jax.experimental import pallas as pl
from jax.experimental.pallas import tpu as pltpu

D_MODEL = 1024
DEPTH = 4
NORM_EPS = 1e-6

A_HEADS = 8
A_QK_DIM = 64
A_ROT_DIM = 16
A_IN = 4096
ROPE_THETA = 500000.0

B_HEADS = 8
B_KV_HEADS = 2
B_GROUP = B_HEADS // B_KV_HEADS
B_HEAD_DIM = 128
B_IN = 2560
B_AXIS_DIM = 64
AXIAL_THETA = 10000.0
GRID_W = 64

LANES = 128
LOG2E = 1.4426950408889634
VMEM_LIMIT_BYTES = 56 * 1024 * 1024

PROJ_ROWS = {"a": 512, "b": 256}
OUT_ROWS = 512
PROJ_COLS = 512
ATTN_ROWS = 512
KEY_CHUNK = 512
B_SPLIT_ROWS = 2048


def _compiler_params(n_axes):
    return pltpu.CompilerParams(
        dimension_semantics=("parallel",) * n_axes,
        vmem_limit_bytes=VMEM_LIMIT_BYTES)


def _angles(pos, dim, theta):
    inv_freq = theta ** (-(jnp.arange(0, dim, 2, dtype=jnp.float32) / dim))
    return pos[:, None] * inv_freq[None, :]


def _rope_tables_a(seq):
    ang = _angles(jnp.arange(seq, dtype=jnp.float32), A_ROT_DIM, ROPE_THETA)
    cos, sin = jnp.cos(ang), jnp.sin(ang)
    half = A_ROT_DIM // 2
    one = jnp.ones((seq, A_QK_DIM - A_ROT_DIM), jnp.float32)
    zero_h = jnp.zeros((seq, half), jnp.float32)
    zero_r = jnp.zeros((seq, A_QK_DIM - A_ROT_DIM), jnp.float32)
    c = jnp.concatenate([cos, cos, one], axis=1)
    sa = jnp.concatenate([-sin, zero_h, zero_r], axis=1)
    sb = jnp.concatenate([zero_h, sin, zero_r], axis=1)
    tile2 = lambda t: jnp.concatenate([t, t], axis=1)
    return tile2(c), tile2(sa), tile2(sb)


def _rope_tables_b(seq):
    rows = seq // GRID_W
    row_ids = jnp.repeat(jnp.arange(rows, dtype=jnp.float32), GRID_W)
    col_ids = jnp.tile(jnp.arange(GRID_W, dtype=jnp.float32), rows)
    ar = _angles(row_ids, B_AXIS_DIM, AXIAL_THETA)
    ac = _angles(col_ids, B_AXIS_DIM, AXIAL_THETA)
    z = jnp.zeros_like(ar)
    c = jnp.concatenate([jnp.cos(ar), jnp.cos(ar), jnp.cos(ac), jnp.cos(ac)], axis=1)
    sa = jnp.concatenate([-jnp.sin(ar), z, -jnp.sin(ac), z], axis=1)
    sb = jnp.concatenate([z, jnp.sin(ar), z, jnp.sin(ac)], axis=1)
    return c, sa, sb


def _rotate(y, c, sa, sb, shift):
    return (y * c + pltpu.roll(y, LANES - shift, axis=1) * sa
            + pltpu.roll(y, shift, axis=1) * sb)


def _silu(g):
    return g / (1.0 + jnp.exp(-g))


def _rms_scale(y):
    return lax.rsqrt(jnp.mean(y * y, axis=-1, keepdims=True) + NORM_EPS)


def _inproj_kernel(x_ref, nw_ref, w_ref, c_ref, sa_ref, sb_ref, qn_ref, kn_ref,
                   o_ref, *, mixer):
    x = x_ref[...]
    h = (x * _rms_scale(x) * nw_ref[...]).astype(jnp.bfloat16)
    c, sa, sb = c_ref[...], sa_ref[...], sb_ref[...]
    n_out = o_ref.shape[1]
    if mixer == "a":
        n_q, n_k, shift = A_HEADS, A_HEADS, A_ROT_DIM // 2
        q_scale = (A_QK_DIM ** -0.5) * LOG2E
        n_plain = A_HEADS
    else:
        n_q, n_k, shift = B_HEADS, B_KV_HEADS, B_AXIS_DIM // 2
        q_scale = (B_HEAD_DIM ** -0.5) * LOG2E
        n_plain = B_KV_HEADS
    for col0 in range(0, n_out, PROJ_COLS):
        y = jnp.dot(h, w_ref[:, col0:col0 + PROJ_COLS],
                    preferred_element_type=jnp.float32)
        for t in range(PROJ_COLS // LANES):
            tile = col0 // LANES + t
            yt = y[:, t * LANES:(t + 1) * LANES]
            if tile < n_q + n_k:
                is_q = tile < n_q
                if mixer == "b":
                    yt = yt * _rms_scale(yt) * (qn_ref[...] if is_q else kn_ref[...])
                yt = _rotate(yt, c, sa, sb, shift)
                if is_q:
                    yt = yt * q_scale
            elif tile >= n_q + n_k + n_plain:
                yt = _silu(yt)
            o_ref[:, tile * LANES:(tile + 1) * LANES] = yt.astype(o_ref.dtype)


def _inproj(x2, seq, norm_w, w_in, tables, qn, kn, mixer):
    n_tok = x2.shape[0]
    n_out = w_in.shape[1]
    rows = PROJ_ROWS[mixer]
    per_seq = seq // rows
    tab_spec = pl.BlockSpec((rows, LANES), lambda i: (i % per_seq, 0))
    vec_spec = pl.BlockSpec((1, LANES), lambda i: (0, 0))
    return pl.pallas_call(
        functools.partial(_inproj_kernel, mixer=mixer),
        grid=(n_tok // rows,),
        in_specs=[
            pl.BlockSpec((rows, D_MODEL), lambda i: (i, 0)),
            pl.BlockSpec((1, D_MODEL), lambda i: (0, 0)),
            pl.BlockSpec((D_MODEL, n_out), lambda i: (0, 0)),
            tab_spec, tab_spec, tab_spec, vec_spec, vec_spec,
        ],
        out_specs=pl.BlockSpec((rows, n_out), lambda i: (i, 0)),
        out_shape=jax.ShapeDtypeStruct((n_tok, n_out), jnp.bfloat16),
        compiler_params=_compiler_params(1),
        name=f"inproj_{mixer}",
    )(x2, norm_w.reshape(1, D_MODEL), w_in, *tables,
      qn.reshape(1, LANES), kn.reshape(1, LANES))


def _lane_fold(x, op):
    out = x[:, 0:LANES]
    for t in range(1, x.shape[1] // LANES):
        out = op(out, x[:, t * LANES:(t + 1) * LANES])
    return out


def _attention_groups(make_qs, finish, n_groups, k_ref, v_ref, vaug_scr, bufs):
    n_chunks, rows, _ = bufs[0][0].shape
    vaug_scr[:, 0:LANES] = v_ref[...]
    vaug_scr[:, LANES:2 * LANES] = jnp.ones((v_ref.shape[0], LANES), vaug_scr.dtype)

    def score_chunk(qs, c, s_scr, m_run):
        s = lax.dot_general(qs, k_ref[c * KEY_CHUNK:(c + 1) * KEY_CHUNK, :],
                            (((1,), (1,)), ((), ())),
                            preferred_element_type=jnp.float32)
        s_scr[c] = s
        m_c = _lane_fold(s, jnp.maximum)
        return m_c if m_run is None else jnp.maximum(m_run, m_c)

    def weigh_chunk(c, s_scr, m_cur, acc):
        s = s_scr[c]
        p = jnp.concatenate(
            [jnp.exp2(s[:, t * LANES:(t + 1) * LANES] - m_cur)
             for t in range(KEY_CHUNK // LANES)], axis=1).astype(jnp.bfloat16)
        pv = jnp.dot(p, vaug_scr[c * KEY_CHUNK:(c + 1) * KEY_CHUNK, :],
                     preferred_element_type=jnp.float32)
        return pv if acc is None else acc + pv

    def stage(g_cur, g_next, parity):
        s_cur, m_ref = bufs[parity]
        s_next, m_next = bufs[1 - parity]
        if g_next is not None:
            qs_next = make_qs(g_next)
        if g_cur is not None:
            m_cur = m_ref[...]
        m_run, acc = None, None
        for c in range(n_chunks):
            if g_next is not None:
                m_run = score_chunk(qs_next, c, s_next, m_run)
            if g_cur is not None:
                acc = weigh_chunk(c, s_cur, m_cur, acc)
        if g_next is not None:
            m = jnp.max(m_run, axis=-1, keepdims=True)
            m_next[...] = jnp.broadcast_to(m, (rows, LANES))
        if g_cur is not None:
            finish(g_cur, acc)

    stage(None, 0, 1)

    def pair(j, carry):
        g = 2 * j
        stage(g, g + 1, 0)
        stage(g + 1, g + 2, 1)
        return carry

    lax.fori_loop(0, n_groups // 2 - 1, pair, 0)
    stage(n_groups - 2, n_groups - 1, 0)
    stage(n_groups - 1, None, 1)


def _row_start(g, bq):
    return g * bq if isinstance(g, int) else pl.multiple_of(g * bq, bq)


def _attn_a_kernel(lam_ref, subln_ref, q_ref, k_ref, v_ref, g_ref, o_ref,
                   vaug_scr, s0_scr, m0_scr, s1_scr, m1_scr, *, lam_init):
    bq = ATTN_ROWS // 2
    n_groups = q_ref.shape[0] // bq
    lam = lam_ref[...]
    lam_full = (jnp.exp(jnp.sum(lam[0:1] * lam[1:2], axis=-1, keepdims=True))
                - jnp.exp(jnp.sum(lam[2:3] * lam[3:4], axis=-1, keepdims=True))
                + lam_init)
    subln = subln_ref[...] * (1.0 - lam_init)

    def make_qs(g):
        q = q_ref[pl.ds(_row_start(g, bq), bq), :]
        lane = lax.broadcasted_iota(jnp.int32, q.shape, 1)
        zero = jnp.zeros_like(q)
        return jnp.concatenate([jnp.where(lane < A_QK_DIM, q, zero),
                                jnp.where(lane >= A_QK_DIM, q, zero)], axis=0)

    def finish(g, acc):
        r0 = _row_start(g, bq)
        o = acc[:, 0:LANES] / acc[:, LANES:2 * LANES]
        o = o[:bq] - lam_full * o[bq:]
        o = o * _rms_scale(o) * subln
        gate = g_ref[pl.ds(r0, bq), :].astype(jnp.float32)
        o_ref[pl.ds(r0, bq), :] = (o * gate).astype(o_ref.dtype)

    _attention_groups(make_qs, finish, n_groups, k_ref, v_ref, vaug_scr,
                      ((s0_scr, m0_scr), (s1_scr, m1_scr)))


def _attn_scratch(seq):
    n_chunks = seq // KEY_CHUNK
    scores = pltpu.VMEM((n_chunks, ATTN_ROWS, KEY_CHUNK), jnp.float32)
    maxima = pltpu.VMEM((ATTN_ROWS, LANES), jnp.float32)
    return [pltpu.VMEM((seq, 2 * LANES), jnp.bfloat16), scores, maxima, scores, maxima]


def _attn_a(proj, batch, seq, lam, subln, lam_init):
    head_spec = lambda tile0: pl.BlockSpec((seq, LANES), lambda b, h: (b, tile0 + h))
    return pl.pallas_call(
        functools.partial(_attn_a_kernel, lam_init=lam_init),
        grid=(batch, A_HEADS),
        in_specs=[
            pl.BlockSpec((4, A_QK_DIM), lambda b, h: (0, 0)),
            pl.BlockSpec((1, LANES), lambda b, h: (0, 0)),
            head_spec(0), head_spec(A_HEADS), head_spec(2 * A_HEADS), head_spec(3 * A_HEADS),
        ],
        out_specs=head_spec(0),
        out_shape=jax.ShapeDtypeStruct((batch * seq, A_HEADS * LANES), jnp.bfloat16),
        scratch_shapes=_attn_scratch(seq),
        compiler_params=_compiler_params(2),
        name="attn_a",
    )(lam, subln.reshape(1, LANES), proj, proj, proj, proj)


def _attn_b_kernel(q_ref, k_ref, v_ref, g_ref, o_ref,
                   vaug_scr, s0_scr, m0_scr, s1_scr, m1_scr):
    bq = ATTN_ROWS // B_GROUP
    n_groups = q_ref.shape[0] // bq

    def make_qs(g):
        r0 = _row_start(g, bq)
        return jnp.concatenate(
            [q_ref[pl.ds(r0, bq), i * LANES:(i + 1) * LANES] for i in range(B_GROUP)], axis=0)

    def finish(g, acc):
        r0 = _row_start(g, bq)
        o = acc[:, 0:LANES] / acc[:, LANES:2 * LANES]
        for i in range(B_GROUP):
            gate = g_ref[pl.ds(r0, bq), i * LANES:(i + 1) * LANES].astype(jnp.float32)
            o_ref[pl.ds(r0, bq), i * LANES:(i + 1) * LANES] = (
                o[i * bq:(i + 1) * bq] * gate).astype(o_ref.dtype)

    _attention_groups(make_qs, finish, n_groups, k_ref, v_ref, vaug_scr,
                      ((s0_scr, m0_scr), (s1_scr, m1_scr)))


def _attn_b(proj, batch, seq):
    q_rows = min(seq, B_SPLIT_ROWS)
    n_split = seq // q_rows
    group_cols = B_GROUP * LANES
    k_tile0 = B_HEADS
    v_tile0 = B_HEADS + B_KV_HEADS
    g_blk0 = (B_HEADS + 2 * B_KV_HEADS) * LANES // group_cols
    q_spec = lambda blk0: pl.BlockSpec(
        (q_rows, group_cols), lambda b, kv, i: (b * n_split + i, blk0 + kv))
    kv_spec = lambda tile0: pl.BlockSpec((seq, LANES), lambda b, kv, i: (b, tile0 + kv))
    return pl.pallas_call(
        _attn_b_kernel,
        grid=(batch, B_KV_HEADS, n_split),
        in_specs=[q_spec(0), kv_spec(k_tile0), kv_spec(v_tile0), q_spec(g_blk0)],
        out_specs=q_spec(0),
        out_shape=jax.ShapeDtypeStruct((batch * seq, B_HEADS * LANES), jnp.bfloat16),
        scratch_shapes=_attn_scratch(seq),
        compiler_params=_compiler_params(3),
        name="attn_b",
    )(proj, proj, proj, proj)


def _outproj_kernel(o_ref, w_ref, nw_ref, x_ref, y_ref):
    m = jnp.dot(o_ref[...], w_ref[...], preferred_element_type=jnp.float32)
    y_ref[...] = x_ref[...] + m * _rms_scale(m) * nw_ref[...]


def _outproj(o, w_out, norm_w, x2):
    n_tok = x2.shape[0]
    row_spec = pl.BlockSpec((OUT_ROWS, D_MODEL), lambda i: (i, 0))
    return pl.pallas_call(
        _outproj_kernel,
        grid=(n_tok // OUT_ROWS,),
        in_specs=[
            row_spec,
            pl.BlockSpec((D_MODEL, D_MODEL), lambda i: (0, 0)),
            pl.BlockSpec((1, D_MODEL), lambda i: (0, 0)),
            row_spec,
        ],
        out_specs=row_spec,
        out_shape=jax.ShapeDtypeStruct((n_tok, D_MODEL), jnp.float32),
        compiler_params=_compiler_params(1),
        name="outproj",
    )(o, w_out, norm_w.reshape(1, D_MODEL), x2)


def _trunk(x, norm_pre, norm_post, a_w_in, a_w_out, a_lam, a_subln,
           b_w_in, b_w_out, b_q_norm, b_k_norm):
    batch, seq, _ = x.shape
    x2 = x.reshape(batch * seq, D_MODEL)
    tables_a = _rope_tables_a(seq)
    tables_b = _rope_tables_b(seq)
    ones = jnp.ones((LANES,), jnp.float32)
    for i in range(DEPTH):
        j = i // 2
        if i % 2 == 0:
            proj = _inproj(x2, seq, norm_pre[i], a_w_in[j], tables_a, ones, ones, "a")
            lam_init = 0.8 - 0.6 * math.exp(-0.3 * i)
            o = _attn_a(proj, batch, seq, a_lam[j], a_subln[j], lam_init)
            x2 = _outproj(o, a_w_out[j], norm_post[i], x2)
        else:
            proj = _inproj(x2, seq, norm_pre[i], b_w_in[j], tables_b,
                           b_q_norm[j], b_k_norm[j], "b")
            o = _attn_b(proj, batch, seq)
            x2 = _outproj(o, b_w_out[j], norm_post[i], x2)
    return x2.reshape(batch, seq, D_MODEL)


def kernel(x_prompt, x_sample, norm_pre, norm_post, a_w_in, a_w_out, a_lam, a_subln,
           b_w_in, b_w_out, b_q_norm, b_k_norm):
    bf16 = jnp.bfloat16
    weights = (norm_pre, norm_post, a_w_in.astype(bf16), a_w_out.astype(bf16), a_lam,
               a_subln, b_w_in.astype(bf16), b_w_out.astype(bf16), b_q_norm, b_k_norm)
    return (_trunk(x_prompt, *weights), _trunk(x_sample, *weights))
```

```python
import functools
import math

import jax
import jax.numpy as jnp
from jax import lax
from jax.experimental import pallas as pl
from jax.experimental.pallas import tpu as pltpu

D_MODEL = 1024
DEPTH = 4
NORM_EPS = 1e-6

A_HEADS = 8
A_QK_DIM = 64
A_ROT_DIM = 16
A_IN = 4096
ROPE_THETA = 500000.0

B_HEADS = 8
B_KV_HEADS = 2
B_GROUP = B_HEADS // B_KV_HEADS
B_HEAD_DIM = 128
B_IN = 2560
B_AXIS_DIM = 64
AXIAL_THETA = 10000.0
GRID_W = 64

LANES = 128
LOG2E = 1.4426950408889634
VMEM_LIMIT_BYTES = 56 * 1024 * 1024

PROJ_ROWS = {"a": 512, "b": 256}
OUT_ROWS = 1024
PROJ_COLS = 512
ATTN_ROWS = 512
KEY_CHUNK = 512
B_SPLIT_ROWS = 2048


def _compiler_params(n_axes):
    return pltpu.CompilerParams(
        dimension_semantics=("parallel",) * n_axes,
        vmem_limit_bytes=VMEM_LIMIT_BYTES)


def _angles(pos, dim, theta):
    inv_freq = theta ** (-(jnp.arange(0, dim, 2, dtype=jnp.float32) / dim))
    return pos[:, None] * inv_freq[None, :]


def _rope_tables_a(seq):
    ang = _angles(jnp.arange(seq, dtype=jnp.float32), A_ROT_DIM, ROPE_THETA)
    cos, sin = jnp.cos(ang), jnp.sin(ang)
    half = A_ROT_DIM // 2
    one = jnp.ones((seq, A_QK_DIM - A_ROT_DIM), jnp.float32)
    zero_h = jnp.zeros((seq, half), jnp.float32)
    zero_r = jnp.zeros((seq, A_QK_DIM - A_ROT_DIM), jnp.float32)
    c = jnp.concatenate([cos, cos, one], axis=1)
    sa = jnp.concatenate([-sin, zero_h, zero_r], axis=1)
    sb = jnp.concatenate([zero_h, sin, zero_r], axis=1)
    tile2 = lambda t: jnp.concatenate([t, t], axis=1)
    return tile2(c), tile2(sa), tile2(sb)


def _rope_tables_b(seq):
    rows = seq // GRID_W
    row_ids = jnp.repeat(jnp.arange(rows, dtype=jnp.float32), GRID_W)
    col_ids = jnp.tile(jnp.arange(GRID_W, dtype=jnp.float32), rows)
    ar = _angles(row_ids, B_AXIS_DIM, AXIAL_THETA)
    ac = _angles(col_ids, B_AXIS_DIM, AXIAL_THETA)
    z = jnp.zeros_like(ar)
    c = jnp.concatenate([jnp.cos(ar), jnp.cos(ar), jnp.cos(ac), jnp.cos(ac)], axis=1)
    sa = jnp.concatenate([-jnp.sin(ar), z, -jnp.sin(ac), z], axis=1)
    sb = jnp.concatenate([z, jnp.sin(ar), z, jnp.sin(ac)], axis=1)
    return c, sa, sb


def _rotate(y, c, sa, sb, shift):
    return (y * c + pltpu.roll(y, LANES - shift, axis=1) * sa
            + pltpu.roll(y, shift, axis=1) * sb)


def _silu(g):
    return g / (1.0 + jnp.exp(-g))


def _rms_scale(y):
    return lax.rsqrt(jnp.mean(y * y, axis=-1, keepdims=True) + NORM_EPS)


def _inproj_kernel(x_ref, nw_ref, w_ref, c_ref, sa_ref, sb_ref, qn_ref, kn_ref,
                   o_ref, *, mixer):
    x = x_ref[...]
    h = (x * _rms_scale(x) * nw_ref[...]).astype(jnp.bfloat16)
    c, sa, sb = c_ref[...], sa_ref[...], sb_ref[...]
    n_out = o_ref.shape[1]
    if mixer == "a":
        n_q, n_k, shift = A_HEADS, A_HEADS, A_ROT_DIM // 2
        q_scale = (A_QK_DIM ** -0.5) * LOG2E
        n_plain = A_HEADS
    else:
        n_q, n_k, shift = B_HEADS, B_KV_HEADS, B_AXIS_DIM // 2
        q_scale = (B_HEAD_DIM ** -0.5) * LOG2E
        n_plain = B_KV_HEADS
    for col0 in range(0, n_out, PROJ_COLS):
        y = jnp.dot(h, w_ref[:, col0:col0 + PROJ_COLS],
                    preferred_element_type=jnp.float32)
        for t in range(PROJ_COLS // LANES):
            tile = col0 // LANES + t
            yt = y[:, t * LANES:(t + 1) * LANES]
            if tile < n_q + n_k:
                is_q = tile < n_q
                if mixer == "b":
                    yt = yt * _rms_scale(yt) * (qn_ref[...] if is_q else kn_ref[...])
                yt = _rotate(yt, c, sa, sb, shift)
                if is_q:
                    yt = yt * q_scale
            elif tile >= n_q + n_k + n_plain:
                yt = _silu(yt)
            o_ref[:, tile * LANES:(tile + 1) * LANES] = yt.astype(o_ref.dtype)


def _inproj(x2, seq, norm_w, w_in, tables, qn, kn, mixer):
    n_tok = x2.shape[0]
    n_out = w_in.shape[1]
    rows = PROJ_ROWS[mixer]
    per_seq = seq // rows
    tab_spec = pl.BlockSpec((rows, LANES), lambda i: (i % per_seq, 0))
    vec_spec = pl.BlockSpec((1, LANES), lambda i: (0, 0))
    return pl.pallas_call(
        functools.partial(_inproj_kernel, mixer=mixer),
        grid=(n_tok // rows,),
        in_specs=[
            pl.BlockSpec((rows, D_MODEL), lambda i: (i, 0)),
            pl.BlockSpec((1, D_MODEL), lambda i: (0, 0)),
            pl.BlockSpec((D_MODEL, n_out), lambda i: (0, 0)),
            tab_spec, tab_spec, tab_spec, vec_spec, vec_spec,
        ],
        out_specs=pl.BlockSpec((rows, n_out), lambda i: (i, 0)),
        out_shape=jax.ShapeDtypeStruct((n_tok, n_out), jnp.bfloat16),
        compiler_params=_compiler_params(1),
        name=f"inproj_{mixer}",
    )(x2, norm_w.reshape(1, D_MODEL), w_in, *tables,
      qn.reshape(1, LANES), kn.reshape(1, LANES))


def _lane_fold(x, op):
    out = x[:, 0:LANES]
    for t in range(1, x.shape[1] // LANES):
        out = op(out, x[:, t * LANES:(t + 1) * LANES])
    return out


def _attention_groups(make_qs, finish, n_groups, k_ref, v_ref, vaug_scr, acc_scr, bufs):
    n_chunks, rows, _ = bufs[0][0].shape
    vaug_scr[:, 0:LANES] = v_ref[...]
    vaug_scr[:, LANES:2 * LANES] = jnp.ones((v_ref.shape[0], LANES), vaug_scr.dtype)

    def score_chunk(qs, c, s_scr, m_run):
        s = lax.dot_general(qs, k_ref[c * KEY_CHUNK:(c + 1) * KEY_CHUNK, :],
                            (((1,), (1,)), ((), ())),
                            preferred_element_type=jnp.float32)
        s_scr[c] = s
        m_c = _lane_fold(s, jnp.maximum)
        return m_c if m_run is None else jnp.maximum(m_run, m_c)

    def weigh_chunk(c, s_scr, m_cur, acc):
        s = s_scr[c]
        p = jnp.concatenate(
            [jnp.exp2(s[:, t * LANES:(t + 1) * LANES] - m_cur)
             for t in range(KEY_CHUNK // LANES)], axis=1).astype(jnp.bfloat16)
        pv = jnp.dot(p, vaug_scr[c * KEY_CHUNK:(c + 1) * KEY_CHUNK, :],
                     preferred_element_type=jnp.float32)
        return pv if acc is None else acc + pv

    def stage(g_cur, g_next, parity, defer_finish=False):
        s_cur, m_ref = bufs[parity]
        s_next, m_next = bufs[1 - parity]
        if g_next is not None:
            qs_next = make_qs(g_next)
        if g_cur is not None:
            m_cur = m_ref[...]
        m_run, acc = None, None
        for c in range(n_chunks):
            if g_next is not None:
                m_run = score_chunk(qs_next, c, s_next, m_run)
            if g_cur is not None:
                acc = weigh_chunk(c, s_cur, m_cur, acc)
        if g_next is not None:
            m = jnp.max(m_run, axis=-1, keepdims=True)
            m_next[...] = jnp.broadcast_to(m, (rows, LANES))
        if g_cur is not None:
            if defer_finish:
                acc_scr[...] = acc
            else:
                finish(g_cur, acc)

    stage(None, 0, 1)
    stage(0, 1, 0, defer_finish=True)

    def pair(j, carry):
        g = 2 * j
        finish(g, acc_scr[...])
        stage(g + 1, g + 2, 1)
        stage(g + 2, g + 3, 0, defer_finish=True)
        return carry

    lax.fori_loop(0, n_groups // 2 - 1, pair, 0)
    finish(n_groups - 2, acc_scr[...])
    stage(n_groups - 1, None, 1)


def _row_start(g, bq):
    return g * bq if isinstance(g, int) else pl.multiple_of(g * bq, bq)


def _attn_a_kernel(lam_ref, subln_ref, q_ref, k_ref, v_ref, g_ref, o_ref,
                   vaug_scr, acc_scr, s0_scr, m0_scr, s1_scr, m1_scr, *, lam_init):
    bq = ATTN_ROWS // 2
    n_groups = q_ref.shape[0] // bq
    lam = lam_ref[...]
    lam_full = (jnp.exp(jnp.sum(lam[0:1] * lam[1:2], axis=-1, keepdims=True))
                - jnp.exp(jnp.sum(lam[2:3] * lam[3:4], axis=-1, keepdims=True))
                + lam_init)
    subln = subln_ref[...] * (1.0 - lam_init)

    def make_qs(g):
        q = q_ref[pl.ds(_row_start(g, bq), bq), :]
        lane = lax.broadcasted_iota(jnp.int32, q.shape, 1)
        zero = jnp.zeros_like(q)
        return jnp.concatenate([jnp.where(lane < A_QK_DIM, q, zero),
                                jnp.where(lane >= A_QK_DIM, q, zero)], axis=0)

    def finish(g, acc):
        r0 = _row_start(g, bq)
        o = acc[:, 0:LANES] / acc[:, LANES:2 * LANES]
        o = o[:bq] - lam_full * o[bq:]
        o = o * _rms_scale(o) * subln
        gate = g_ref[pl.ds(r0, bq), :].astype(jnp.float32)
        o_ref[pl.ds(r0, bq), :] = (o * gate).astype(o_ref.dtype)

    _attention_groups(make_qs, finish, n_groups, k_ref, v_ref, vaug_scr, acc_scr,
                      ((s0_scr, m0_scr), (s1_scr, m1_scr)))


def _attn_scratch(seq):
    n_chunks = seq // KEY_CHUNK
    scores = pltpu.VMEM((n_chunks, ATTN_ROWS, KEY_CHUNK), jnp.float32)
    maxima = pltpu.VMEM((ATTN_ROWS, LANES), jnp.float32)
    return [pltpu.VMEM((seq, 2 * LANES), jnp.bfloat16),
            pltpu.VMEM((ATTN_ROWS, 2 * LANES), jnp.float32),
            scores, maxima, scores, maxima]


def _attn_a(proj, batch, seq, lam, subln, lam_init):
    head_spec = lambda tile0: pl.BlockSpec((seq, LANES), lambda b, h: (b, tile0 + h))
    return pl.pallas_call(
        functools.partial(_attn_a_kernel, lam_init=lam_init),
        grid=(batch, A_HEADS),
        in_specs=[
            pl.BlockSpec((4, A_QK_DIM), lambda b, h: (0, 0)),
            pl.BlockSpec((1, LANES), lambda b, h: (0, 0)),
            head_spec(0), head_spec(A_HEADS), head_spec(2 * A_HEADS), head_spec(3 * A_HEADS),
        ],
        out_specs=head_spec(0),
        out_shape=jax.ShapeDtypeStruct((batch * seq, A_HEADS * LANES), jnp.bfloat16),
        scratch_shapes=_attn_scratch(seq),
        compiler_params=_compiler_params(2),
        name="attn_a",
    )(lam, subln.reshape(1, LANES), proj, proj, proj, proj)


def _attn_b_kernel(q_ref, k_ref, v_ref, g_ref, o_ref,
                   vaug_scr, acc_scr, s0_scr, m0_scr, s1_scr, m1_scr):
    bq = ATTN_ROWS // B_GROUP
    n_groups = q_ref.shape[0] // bq

    def make_qs(g):
        r0 = _row_start(g, bq)
        return jnp.concatenate(
            [q_ref[pl.ds(r0, bq), i * LANES:(i + 1) * LANES] for i in range(B_GROUP)], axis=0)

    def finish(g, acc):
        r0 = _row_start(g, bq)
        o = acc[:, 0:LANES] / acc[:, LANES:2 * LANES]
        for i in range(B_GROUP):
            gate = g_ref[pl.ds(r0, bq), i * LANES:(i + 1) * LANES].astype(jnp.float32)
            o_ref[pl.ds(r0, bq), i * LANES:(i + 1) * LANES] = (
                o[i * bq:(i + 1) * bq] * gate).astype(o_ref.dtype)

    _attention_groups(make_qs, finish, n_groups, k_ref, v_ref, vaug_scr, acc_scr,
                      ((s0_scr, m0_scr), (s1_scr, m1_scr)))


def _attn_b(proj, batch, seq):
    q_rows = min(seq, B_SPLIT_ROWS)
    n_split = seq // q_rows
    group_cols = B_GROUP * LANES
    k_tile0 = B_HEADS
    v_tile0 = B_HEADS + B_KV_HEADS
    g_blk0 = (B_HEADS + 2 * B_KV_HEADS) * LANES // group_cols
    q_spec = lambda blk0: pl.BlockSpec(
        (q_rows, group_cols), lambda b, kv, i: (b * n_split + i, blk0 + kv))
    kv_spec = lambda tile0: pl.BlockSpec((seq, LANES), lambda b, kv, i: (b, tile0 + kv))
    return pl.pallas_call(
        _attn_b_kernel,
        grid=(batch, B_KV_HEADS, n_split),
        in_specs=[q_spec(0), kv_spec(k_tile0), kv_spec(v_tile0), q_spec(g_blk0)],
        out_specs=q_spec(0),
        out_shape=jax.ShapeDtypeStruct((batch * seq, B_HEADS * LANES), jnp.bfloat16),
        scratch_shapes=_attn_scratch(seq),
        compiler_params=_compiler_params(3),
        name="attn_b",
    )(proj, proj, proj, proj)


def _outproj_kernel(o_ref, w_ref, nw_ref, x_ref, y_ref):
    m = jnp.dot(o_ref[...], w_ref[...], preferred_element_type=jnp.float32)
    y_ref[...] = x_ref[...] + m * _rms_scale(m) * nw_ref[...]


def _outproj(o, w_out, norm_w, x2):
    n_tok = x2.shape[0]
    row_spec = pl.BlockSpec((OUT_ROWS, D_MODEL), lambda i: (i, 0))
    return pl.pallas_call(
        _outproj_kernel,
        grid=(n_tok // OUT_ROWS,),
        in_specs=[
            row_spec,
            pl.BlockSpec((D_MODEL, D_MODEL), lambda i: (0, 0)),
            pl.BlockSpec((1, D_MODEL), lambda i: (0, 0)),
            row_spec,
        ],
        out_specs=row_spec,
        out_shape=jax.ShapeDtypeStruct((n_tok, D_MODEL), jnp.float32),
        compiler_params=_compiler_params(1),
        name="outproj",
    )(o, w_out, norm_w.reshape(1, D_MODEL), x2)


def _trunk(x, norm_pre, norm_post, a_w_in, a_w_out, a_lam, a_subln,
           b_w_in, b_w_out, b_q_norm, b_k_norm):
    batch, seq, _ = x.shape
    x2 = x.reshape(batch * seq, D_MODEL)
    tables_a = _rope_tables_a(seq)
    tables_b = _rope_tables_b(seq)
    ones = jnp.ones((LANES,), jnp.float32)
    for i in range(DEPTH):
        j = i // 2
        if i % 2 == 0:
            proj = _inproj(x2, seq, norm_pre[i], a_w_in[j], tables_a, ones, ones, "a")
            lam_init = 0.8 - 0.6 * math.exp(-0.3 * i)
            o = _attn_a(proj, batch, seq, a_lam[j], a_subln[j], lam_init)
            x2 = _outproj(o, a_w_out[j], norm_post[i], x2)
        else:
            proj = _inproj(x2, seq, norm_pre[i], b_w_in[j], tables_b,
                           b_q_norm[j], b_k_norm[j], "b")
            o = _attn_b(proj, batch, seq)
            x2 = _outproj(o, b_w_out[j], norm_post[i], x2)
    return x2.reshape(batch, seq, D_MODEL)


def kernel(x_prompt, x_sample, norm_pre, norm_post, a_w_in, a_w_out, a_lam, a_subln,
           b_w_in, b_w_out, b_q_norm, b_k_norm):
    bf16 = jnp.bfloat16
    weights = (norm_pre, norm_post, a_w_in.astype(bf16), a_w_out.astype(bf16), a_lam,
               a_subln, b_w_in.astype(bf16), b_w_out.astype(bf16), b_q_norm, b_k_norm)
    return (_trunk(x_prompt, *weights), _trunk(x_sample, *weights))
```

```python
import functools
import math

import jax
import jax.numpy as jnp
from jax import lax
from jax.experimental import pallas as pl
from jax.experimental.pallas import tpu as pltpu

D_MODEL = 1024
DEPTH = 4
NORM_EPS = 1e-6

A_HEADS = 8
A_QK_DIM = 64
A_ROT_DIM = 16
A_IN = 4096
ROPE_THETA = 500000.0

B_HEADS = 8
B_KV_HEADS = 2
B_GROUP = B_HEADS // B_KV_HEADS
B_HEAD_DIM = 128
B_IN = 2560
B_AXIS_DIM = 64
AXIAL_THETA = 10000.0
GRID_W = 64

LANES = 128
LOG2E = 1.4426950408889634
VMEM_LIMIT_BYTES = 56 * 1024 * 1024

PROJ_ROWS = {"a": 512, "b": 256}
OUT_ROWS = 1024
PROJ_COLS = 512
ATTN_ROWS = 512
KEY_CHUNK = 512
B_SPLIT_ROWS = 2048


def _compiler_params(n_axes):
    return pltpu.CompilerParams(
        dimension_semantics=("parallel",) * n_axes,
        vmem_limit_bytes=VMEM_LIMIT_BYTES)


def _angles(pos, dim, theta):
    inv_freq = theta ** (-(jnp.arange(0, dim, 2, dtype=jnp.float32) / dim))
    return pos[:, None] * inv_freq[None, :]


def _rope_tables_a(seq):
    ang = _angles(jnp.arange(seq, dtype=jnp.float32), A_ROT_DIM, ROPE_THETA)
    cos, sin = jnp.cos(ang), jnp.sin(ang)
    half = A_ROT_DIM // 2
    one = jnp.ones((seq, A_QK_DIM - A_ROT_DIM), jnp.float32)
    zero_h = jnp.zeros((seq, half), jnp.float32)
    zero_r = jnp.zeros((seq, A_QK_DIM - A_ROT_DIM), jnp.float32)
    c = jnp.concatenate([cos, cos, one], axis=1)
    sa = jnp.concatenate([-sin, zero_h, zero_r], axis=1)
    sb = jnp.concatenate([zero_h, sin, zero_r], axis=1)
    tile2 = lambda t: jnp.concatenate([t, t], axis=1)
    return tile2(c), tile2(sa), tile2(sb)


def _rope_tables_b(seq):
    rows = seq // GRID_W
    row_ids = jnp.repeat(jnp.arange(rows, dtype=jnp.float32), GRID_W)
    col_ids = jnp.tile(jnp.arange(GRID_W, dtype=jnp.float32), rows)
    ar = _angles(row_ids, B_AXIS_DIM, AXIAL_THETA)
    ac = _angles(col_ids, B_AXIS_DIM, AXIAL_THETA)
    z = jnp.zeros_like(ar)
    c = jnp.concatenate([jnp.cos(ar), jnp.cos(ar), jnp.cos(ac), jnp.cos(ac)], axis=1)
    sa = jnp.concatenate([-jnp.sin(ar), z, -jnp.sin(ac), z], axis=1)
    sb = jnp.concatenate([z, jnp.sin(ar), z, jnp.sin(ac)], axis=1)
    return c, sa, sb


def _rotate(y, c, sa, sb, shift):
    return (y * c + pltpu.roll(y, LANES - shift, axis=1) * sa
            + pltpu.roll(y, shift, axis=1) * sb)


def _silu(g):
    return g / (1.0 + jnp.exp(-g))


def _rms_scale(y):
    return lax.rsqrt(jnp.mean(y * y, axis=-1, keepdims=True) + NORM_EPS)


def _inproj_kernel(x_ref, nw_ref, w_ref, c_ref, sa_ref, sb_ref, qn_ref, kn_ref,
                   o_ref, *, mixer):
    _inproj_rows(x_ref[...], nw_ref, w_ref, c_ref, sa_ref, sb_ref, qn_ref, kn_ref,
                 o_ref, mixer)


def _inproj_rows(x, nw_ref, w_ref, c_ref, sa_ref, sb_ref, qn_ref, kn_ref, o_ref, mixer):
    h = (x * _rms_scale(x) * nw_ref[...]).astype(jnp.bfloat16)
    c, sa, sb = c_ref[...], sa_ref[...], sb_ref[...]
    n_out = o_ref.shape[1]
    if mixer == "a":
        n_q, n_k, shift = A_HEADS, A_HEADS, A_ROT_DIM // 2
        q_scale = (A_QK_DIM ** -0.5) * LOG2E
        n_plain = A_HEADS
    else:
        n_q, n_k, shift = B_HEADS, B_KV_HEADS, B_AXIS_DIM // 2
        q_scale = (B_HEAD_DIM ** -0.5) * LOG2E
        n_plain = B_KV_HEADS
    for col0 in range(0, n_out, PROJ_COLS):
        y = jnp.dot(h, w_ref[:, col0:col0 + PROJ_COLS],
                    preferred_element_type=jnp.float32)
        for t in range(PROJ_COLS // LANES):
            tile = col0 // LANES + t
            yt = y[:, t * LANES:(t + 1) * LANES]
            if tile < n_q + n_k:
                is_q = tile < n_q
                if mixer == "b":
                    yt = yt * _rms_scale(yt) * (qn_ref[...] if is_q else kn_ref[...])
                yt = _rotate(yt, c, sa, sb, shift)
                if is_q:
                    yt = yt * q_scale
            elif tile >= n_q + n_k + n_plain:
                yt = _silu(yt)
            o_ref[:, tile * LANES:(tile + 1) * LANES] = yt.astype(o_ref.dtype)


def _inproj(x2, seq, norm_w, w_in, tables, qn, kn, mixer):
    n_tok = x2.shape[0]
    n_out = w_in.shape[1]
    rows = PROJ_ROWS[mixer]
    per_seq = seq // rows
    tab_spec = pl.BlockSpec((rows, LANES), lambda i: (i % per_seq, 0))
    vec_spec = pl.BlockSpec((1, LANES), lambda i: (0, 0))
    return pl.pallas_call(
        functools.partial(_inproj_kernel, mixer=mixer),
        grid=(n_tok // rows,),
        in_specs=[
            pl.BlockSpec((rows, D_MODEL), lambda i: (i, 0)),
            pl.BlockSpec((1, D_MODEL), lambda i: (0, 0)),
            pl.BlockSpec((D_MODEL, n_out), lambda i: (0, 0)),
            tab_spec, tab_spec, tab_spec, vec_spec, vec_spec,
        ],
        out_specs=pl.BlockSpec((rows, n_out), lambda i: (i, 0)),
        out_shape=jax.ShapeDtypeStruct((n_tok, n_out), jnp.bfloat16),
        compiler_params=_compiler_params(1),
        name=f"inproj_{mixer}",
    )(x2, norm_w.reshape(1, D_MODEL), w_in, *tables,
      qn.reshape(1, LANES), kn.reshape(1, LANES))


def _lane_fold(x, op):
    out = x[:, 0:LANES]
    for t in range(1, x.shape[1] // LANES):
        out = op(out, x[:, t * LANES:(t + 1) * LANES])
    return out


def _attention_groups(make_qs, finish, n_groups, k_ref, v_ref, vaug_scr, acc_scr, bufs):
    n_chunks, rows, _ = bufs[0][0].shape
    vaug_scr[:, 0:LANES] = v_ref[...]
    vaug_scr[:, LANES:2 * LANES] = jnp.ones((v_ref.shape[0], LANES), vaug_scr.dtype)

    def score_chunk(qs, c, s_scr, m_run):
        s = lax.dot_general(qs, k_ref[c * KEY_CHUNK:(c + 1) * KEY_CHUNK, :],
                            (((1,), (1,)), ((), ())),
                            preferred_element_type=jnp.float32)
        s_scr[c] = s
        m_c = _lane_fold(s, jnp.maximum)
        return m_c if m_run is None else jnp.maximum(m_run, m_c)

    def weigh_chunk(c, s_scr, m_cur, acc):
        s = s_scr[c]
        p = jnp.concatenate(
            [jnp.exp2(s[:, t * LANES:(t + 1) * LANES] - m_cur)
             for t in range(KEY_CHUNK // LANES)], axis=1).astype(jnp.bfloat16)
        pv = jnp.dot(p, vaug_scr[c * KEY_CHUNK:(c + 1) * KEY_CHUNK, :],
                     preferred_element_type=jnp.float32)
        return pv if acc is None else acc + pv

    def stage(g_cur, g_next, parity, defer_finish=False):
        s_cur, m_ref = bufs[parity]
        s_next, m_next = bufs[1 - parity]
        if g_next is not None:
            qs_next = make_qs(g_next)
        if g_cur is not None:
            m_cur = m_ref[...]
        m_run, acc = None, None
        for c in range(n_chunks):
            if g_next is not None:
                m_run = score_chunk(qs_next, c, s_next, m_run)
            if g_cur is not None:
                acc = weigh_chunk(c, s_cur, m_cur, acc)
        if g_next is not None:
            m = jnp.max(m_run, axis=-1, keepdims=True)
            m_next[...] = jnp.broadcast_to(m, (rows, LANES))
        if g_cur is not None:
            if defer_finish:
                acc_scr[...] = acc
            else:
                finish(g_cur, acc)

    stage(None, 0, 1)
    stage(0, 1, 0, defer_finish=True)

    def pair(j, carry):
        g = 2 * j
        finish(g, acc_scr[...])
        stage(g + 1, g + 2, 1)
        stage(g + 2, g + 3, 0, defer_finish=True)
        return carry

    lax.fori_loop(0, n_groups // 2 - 1, pair, 0)
    finish(n_groups - 2, acc_scr[...])
    stage(n_groups - 1, None, 1)


def _row_start(g, bq):
    return g * bq if isinstance(g, int) else pl.multiple_of(g * bq, bq)


def _attn_a_kernel(lam_ref, subln_ref, q_ref, k_ref, v_ref, g_ref, o_ref,
                   vaug_scr, acc_scr, s0_scr, m0_scr, s1_scr, m1_scr, *, lam_init):
    bq = ATTN_ROWS // 2
    n_groups = q_ref.shape[0] // bq
    lam = lam_ref[...]
    lam_full = (jnp.exp(jnp.sum(lam[0:1] * lam[1:2], axis=-1, keepdims=True))
                - jnp.exp(jnp.sum(lam[2:3] * lam[3:4], axis=-1, keepdims=True))
                + lam_init)
    subln = subln_ref[...] * (1.0 - lam_init)

    def make_qs(g):
        q = q_ref[pl.ds(_row_start(g, bq), bq), :]
        lane = lax.broadcasted_iota(jnp.int32, q.shape, 1)
        zero = jnp.zeros_like(q)
        return jnp.concatenate([jnp.where(lane < A_QK_DIM, q, zero),
                                jnp.where(lane >= A_QK_DIM, q, zero)], axis=0)

    def finish(g, acc):
        r0 = _row_start(g, bq)
        o = acc[:, 0:LANES] / acc[:, LANES:2 * LANES]
        o = o[:bq] - lam_full * o[bq:]
        o = o * _rms_scale(o) * subln
        gate = g_ref[pl.ds(r0, bq), :].astype(jnp.float32)
        o_ref[pl.ds(r0, bq), :] = (o * gate).astype(o_ref.dtype)

    _attention_groups(make_qs, finish, n_groups, k_ref, v_ref, vaug_scr, acc_scr,
                      ((s0_scr, m0_scr), (s1_scr, m1_scr)))


def _attn_scratch(seq):
    n_chunks = seq // KEY_CHUNK
    scores = pltpu.VMEM((n_chunks, ATTN_ROWS, KEY_CHUNK), jnp.float32)
    maxima = pltpu.VMEM((ATTN_ROWS, LANES), jnp.float32)
    return [pltpu.VMEM((seq, 2 * LANES), jnp.bfloat16),
            pltpu.VMEM((ATTN_ROWS, 2 * LANES), jnp.float32),
            scores, maxima, scores, maxima]


def _attn_a(proj, batch, seq, lam, subln, lam_init):
    head_spec = lambda tile0: pl.BlockSpec((seq, LANES), lambda b, h: (b, tile0 + h))
    return pl.pallas_call(
        functools.partial(_attn_a_kernel, lam_init=lam_init),
        grid=(batch, A_HEADS),
        in_specs=[
            pl.BlockSpec((4, A_QK_DIM), lambda b, h: (0, 0)),
            pl.BlockSpec((1, LANES), lambda b, h: (0, 0)),
            head_spec(0), head_spec(A_HEADS), head_spec(2 * A_HEADS), head_spec(3 * A_HEADS),
        ],
        out_specs=head_spec(0),
        out_shape=jax.ShapeDtypeStruct((batch * seq, A_HEADS * LANES), jnp.bfloat16),
        scratch_shapes=_attn_scratch(seq),
        compiler_params=_compiler_params(2),
        name="attn_a",
    )(lam, subln.reshape(1, LANES), proj, proj, proj, proj)


def _attn_b_kernel(q_ref, k_ref, v_ref, g_ref, o_ref,
                   vaug_scr, acc_scr, s0_scr, m0_scr, s1_scr, m1_scr):
    bq = ATTN_ROWS // B_GROUP
    n_groups = q_ref.shape[0] // bq

    def make_qs(g):
        r0 = _row_start(g, bq)
        return jnp.concatenate(
            [q_ref[pl.ds(r0, bq), i * LANES:(i + 1) * LANES] for i in range(B_GROUP)], axis=0)

    def finish(g, acc):
        r0 = _row_start(g, bq)
        o = acc[:, 0:LANES] / acc[:, LANES:2 * LANES]
        for i in range(B_GROUP):
            gate = g_ref[pl.ds(r0, bq), i * LANES:(i + 1) * LANES].astype(jnp.float32)
            o_ref[pl.ds(r0, bq), i * LANES:(i + 1) * LANES] = (
                o[i * bq:(i + 1) * bq] * gate).astype(o_ref.dtype)

    _attention_groups(make_qs, finish, n_groups, k_ref, v_ref, vaug_scr, acc_scr,
                      ((s0_scr, m0_scr), (s1_scr, m1_scr)))


def _attn_b(proj, batch, seq):
    q_rows = min(seq, B_SPLIT_ROWS)
    n_split = seq // q_rows
    group_cols = B_GROUP * LANES
    k_tile0 = B_HEADS
    v_tile0 = B_HEADS + B_KV_HEADS
    g_blk0 = (B_HEADS + 2 * B_KV_HEADS) * LANES // group_cols
    q_spec = lambda blk0: pl.BlockSpec(
        (q_rows, group_cols), lambda b, kv, i: (b * n_split + i, blk0 + kv))
    kv_spec = lambda tile0: pl.BlockSpec((seq, LANES), lambda b, kv, i: (b, tile0 + kv))
    return pl.pallas_call(
        _attn_b_kernel,
        grid=(batch, B_KV_HEADS, n_split),
        in_specs=[q_spec(0), kv_spec(k_tile0), kv_spec(v_tile0), q_spec(g_blk0)],
        out_specs=q_spec(0),
        out_shape=jax.ShapeDtypeStruct((batch * seq, B_HEADS * LANES), jnp.bfloat16),
        scratch_shapes=_attn_scratch(seq),
        compiler_params=_compiler_params(3),
        name="attn_b",
    )(proj, proj, proj, proj)


def _outproj_rows(o_ref, w_ref, nw_ref, x_ref):
    m = jnp.dot(o_ref[...], w_ref[...], preferred_element_type=jnp.float32)
    return x_ref[...] + m * _rms_scale(m) * nw_ref[...]


def _outproj_kernel(o_ref, w_ref, nw_ref, x_ref, y_ref):
    y_ref[...] = _outproj_rows(o_ref, w_ref, nw_ref, x_ref)


def _out_in_kernel(o_ref, wo_ref, nwo_ref, x_ref, nwi_ref, wi_ref, c_ref, sa_ref, sb_ref,
                   qn_ref, kn_ref, y_ref, p_ref, *, mixer):
    y = _outproj_rows(o_ref, wo_ref, nwo_ref, x_ref)
    y_ref[...] = y
    _inproj_rows(y, nwi_ref, wi_ref, c_ref, sa_ref, sb_ref, qn_ref, kn_ref, p_ref, mixer)


def _out_in(o, w_out, norm_post, x2, seq, norm_pre, w_in, tables, qn, kn, mixer):
    n_tok = x2.shape[0]
    n_out = w_in.shape[1]
    rows = PROJ_ROWS[mixer]
    per_seq = seq // rows
    row_spec = pl.BlockSpec((rows, D_MODEL), lambda i: (i, 0))
    const = lambda shape: pl.BlockSpec(shape, lambda i: (0, 0))
    tab_spec = pl.BlockSpec((rows, LANES), lambda i: (i % per_seq, 0))
    return pl.pallas_call(
        functools.partial(_out_in_kernel, mixer=mixer),
        grid=(n_tok // rows,),
        in_specs=[
            row_spec, const((D_MODEL, D_MODEL)), const((1, D_MODEL)), row_spec,
            const((1, D_MODEL)), const((D_MODEL, n_out)),
            tab_spec, tab_spec, tab_spec, const((1, LANES)), const((1, LANES)),
        ],
        out_specs=[row_spec, pl.BlockSpec((rows, n_out), lambda i: (i, 0))],
        out_shape=[jax.ShapeDtypeStruct((n_tok, D_MODEL), jnp.float32),
                   jax.ShapeDtypeStruct((n_tok, n_out), jnp.bfloat16)],
        compiler_params=_compiler_params(1),
        name=f"out_in_{mixer}",
    )(o, w_out, norm_post.reshape(1, D_MODEL), x2, norm_pre.reshape(1, D_MODEL), w_in,
      *tables, qn.reshape(1, LANES), kn.reshape(1, LANES))


def _outproj(o, w_out, norm_w, x2):
    n_tok = x2.shape[0]
    row_spec = pl.BlockSpec((OUT_ROWS, D_MODEL), lambda i: (i, 0))
    return pl.pallas_call(
        _outproj_kernel,
        grid=(n_tok // OUT_ROWS,),
        in_specs=[
            row_spec,
            pl.BlockSpec((D_MODEL, D_MODEL), lambda i: (0, 0)),
            pl.BlockSpec((1, D_MODEL), lambda i: (0, 0)),
            row_spec,
        ],
        out_specs=row_spec,
        out_shape=jax.ShapeDtypeStruct((n_tok, D_MODEL), jnp.float32),
        compiler_params=_compiler_params(1),
        name="outproj",
    )(o, w_out, norm_w.reshape(1, D_MODEL), x2)


def _trunk(x, norm_pre, norm_post, a_w_in, a_w_out, a_lam, a_subln,
           b_w_in, b_w_out, b_q_norm, b_k_norm):
    batch, seq, _ = x.shape
    x2 = x.reshape(batch * seq, D_MODEL)
    tables_a = _rope_tables_a(seq)
    tables_b = _rope_tables_b(seq)
    ones = jnp.ones((LANES,), jnp.float32)

    def inproj_args(i):
        j = i // 2
        if i % 2 == 0:
            return (seq, norm_pre[i], a_w_in[j], tables_a, ones, ones, "a")
        return (seq, norm_pre[i], b_w_in[j], tables_b, b_q_norm[j], b_k_norm[j], "b")

    proj = _inproj(x2, *inproj_args(0))
    for i in range(DEPTH):
        j = i // 2
        if i % 2 == 0:
            lam_init = 0.8 - 0.6 * math.exp(-0.3 * i)
            o = _attn_a(proj, batch, seq, a_lam[j], a_subln[j], lam_init)
            w_out = a_w_out[j]
        else:
            o = _attn_b(proj, batch, seq)
            w_out = b_w_out[j]
        if i + 1 < DEPTH:
            x2, proj = _out_in(o, w_out, norm_post[i], x2, *inproj_args(i + 1))
        else:
            x2 = _outproj(o, w_out, norm_post[i], x2)
    return x2.reshape(batch, seq, D_MODEL)


def kernel(x_prompt, x_sample, norm_pre, norm_post, a_w_in, a_w_out, a_lam, a_subln,
           b_w_in, b_w_out, b_q_norm, b_k_norm):
    bf16 = jnp.bfloat16
    weights = (norm_pre, norm_post, a_w_in.astype(bf16), a_w_out.astype(bf16), a_lam,
               a_subln, b_w_in.astype(bf16), b_w_out.astype(bf16), b_q_norm, b_k_norm)
    return (_trunk(x_prompt, *weights), _trunk(x_sample, *weights))
```

```python
import functools
import math

import jax
import jax.numpy as jnp
from jax import lax
from jax.experimental import pallas as pl
from jax.experimental.pallas import tpu as pltpu

D_MODEL = 1024
DEPTH = 4
NORM_EPS = 1e-6

A_HEADS = 8
A_QK_DIM = 64
A_ROT_DIM = 16
A_IN = 4096
ROPE_THETA = 500000.0

B_HEADS = 8
B_KV_HEADS = 2
B_GROUP = B_HEADS // B_KV_HEADS
B_HEAD_DIM = 128
B_IN = 2560
B_AXIS_DIM = 64
AXIAL_THETA = 10000.0
GRID_W = 64

LANES = 128
LOG2E = 1.4426950408889634
VMEM_LIMIT_BYTES = 56 * 1024 * 1024

PROJ_ROWS = {"a": 512, "b": 256}
OUT_ROWS = 1024
PROJ_COLS = 512
ATTN_ROWS = 512
KEY_CHUNK = 512
GROUP_UNROLL = 4
B_SPLIT_ROWS = 2048


def _compiler_params(n_axes):
    return pltpu.CompilerParams(
        dimension_semantics=("parallel",) * n_axes,
        vmem_limit_bytes=VMEM_LIMIT_BYTES)


def _angles(pos, dim, theta):
    inv_freq = theta ** (-(jnp.arange(0, dim, 2, dtype=jnp.float32) / dim))
    return pos[:, None] * inv_freq[None, :]


def _rope_tables_a(seq):
    ang = _angles(jnp.arange(seq, dtype=jnp.float32), A_ROT_DIM, ROPE_THETA)
    cos, sin = jnp.cos(ang), jnp.sin(ang)
    half = A_ROT_DIM // 2
    one = jnp.ones((seq, A_QK_DIM - A_ROT_DIM), jnp.float32)
    zero_h = jnp.zeros((seq, half), jnp.float32)
    zero_r = jnp.zeros((seq, A_QK_DIM - A_ROT_DIM), jnp.float32)
    c = jnp.concatenate([cos, cos, one], axis=1)
    sa = jnp.concatenate([-sin, zero_h, zero_r], axis=1)
    sb = jnp.concatenate([zero_h, sin, zero_r], axis=1)
    tile2 = lambda t: jnp.concatenate([t, t], axis=1)
    return tile2(c), tile2(sa), tile2(sb)


def _rope_tables_b(seq):
    rows = seq // GRID_W
    row_ids = jnp.repeat(jnp.arange(rows, dtype=jnp.float32), GRID_W)
    col_ids = jnp.tile(jnp.arange(GRID_W, dtype=jnp.float32), rows)
    ar = _angles(row_ids, B_AXIS_DIM, AXIAL_THETA)
    ac = _angles(col_ids, B_AXIS_DIM, AXIAL_THETA)
    z = jnp.zeros_like(ar)
    c = jnp.concatenate([jnp.cos(ar), jnp.cos(ar), jnp.cos(ac), jnp.cos(ac)], axis=1)
    sa = jnp.concatenate([-jnp.sin(ar), z, -jnp.sin(ac), z], axis=1)
    sb = jnp.concatenate([z, jnp.sin(ar), z, jnp.sin(ac)], axis=1)
    return c, sa, sb


def _rotate(y, c, sa, sb, shift):
    return (y * c + pltpu.roll(y, LANES - shift, axis=1) * sa
            + pltpu.roll(y, shift, axis=1) * sb)


def _silu(g):
    return g / (1.0 + jnp.exp(-g))


def _rms_scale(y):
    return lax.rsqrt(jnp.mean(y * y, axis=-1, keepdims=True) + NORM_EPS)


def _inproj_kernel(x_ref, nw_ref, w_ref, c_ref, sa_ref, sb_ref, qn_ref, kn_ref,
                   o_ref, *, mixer):
    _inproj_rows(x_ref[...], nw_ref, w_ref, c_ref, sa_ref, sb_ref, qn_ref, kn_ref,
                 o_ref, mixer)


def _inproj_rows(x, nw_ref, w_ref, c_ref, sa_ref, sb_ref, qn_ref, kn_ref, o_ref, mixer):
    h = (x * _rms_scale(x) * nw_ref[...]).astype(jnp.bfloat16)
    c, sa, sb = c_ref[...], sa_ref[...], sb_ref[...]
    n_out = o_ref.shape[1]
    if mixer == "a":
        n_q, n_k, shift = A_HEADS, A_HEADS, A_ROT_DIM // 2
        q_scale = (A_QK_DIM ** -0.5) * LOG2E
        n_plain = A_HEADS
    else:
        n_q, n_k, shift = B_HEADS, B_KV_HEADS, B_AXIS_DIM // 2
        q_scale = (B_HEAD_DIM ** -0.5) * LOG2E
        n_plain = B_KV_HEADS
    for col0 in range(0, n_out, PROJ_COLS):
        y = jnp.dot(h, w_ref[:, col0:col0 + PROJ_COLS],
                    preferred_element_type=jnp.float32)
        for t in range(PROJ_COLS // LANES):
            tile = col0 // LANES + t
            yt = y[:, t * LANES:(t + 1) * LANES]
            if tile < n_q + n_k:
                is_q = tile < n_q
                if mixer == "b":
                    yt = yt * _rms_scale(yt) * (qn_ref[...] if is_q else kn_ref[...])
                yt = _rotate(yt, c, sa, sb, shift)
                if is_q:
                    yt = yt * q_scale
            elif tile >= n_q + n_k + n_plain:
                yt = _silu(yt)
            o_ref[:, tile * LANES:(tile + 1) * LANES] = yt.astype(o_ref.dtype)


def _inproj(x2, seq, norm_w, w_in, tables, qn, kn, mixer):
    n_tok = x2.shape[0]
    n_out = w_in.shape[1]
    rows = PROJ_ROWS[mixer]
    per_seq = seq // rows
    tab_spec = pl.BlockSpec((rows, LANES), lambda i: (i % per_seq, 0))
    vec_spec = pl.BlockSpec((1, LANES), lambda i: (0, 0))
    return pl.pallas_call(
        functools.partial(_inproj_kernel, mixer=mixer),
        grid=(n_tok // rows,),
        in_specs=[
            pl.BlockSpec((rows, D_MODEL), lambda i: (i, 0)),
            pl.BlockSpec((1, D_MODEL), lambda i: (0, 0)),
            pl.BlockSpec((D_MODEL, n_out), lambda i: (0, 0)),
            tab_spec, tab_spec, tab_spec, vec_spec, vec_spec,
        ],
        out_specs=pl.BlockSpec((rows, n_out), lambda i: (i, 0)),
        out_shape=jax.ShapeDtypeStruct((n_tok, n_out), jnp.bfloat16),
        compiler_params=_compiler_params(1),
        name=f"inproj_{mixer}",
    )(x2, norm_w.reshape(1, D_MODEL), w_in, *tables,
      qn.reshape(1, LANES), kn.reshape(1, LANES))


def _lane_fold(x, op):
    out = x[:, 0:LANES]
    for t in range(1, x.shape[1] // LANES):
        out = op(out, x[:, t * LANES:(t + 1) * LANES])
    return out


def _attention_groups(make_qs, finish, n_groups, k_ref, v_ref, vaug_scr, acc_scr, bufs):
    n_chunks, rows, _ = bufs[0][0].shape
    vaug_scr[:, 0:LANES] = v_ref[...]
    vaug_scr[:, LANES:2 * LANES] = jnp.ones((v_ref.shape[0], LANES), vaug_scr.dtype)

    def score_chunk(qs, c, s_scr, m_run):
        s = lax.dot_general(qs, k_ref[c * KEY_CHUNK:(c + 1) * KEY_CHUNK, :],
                            (((1,), (1,)), ((), ())),
                            preferred_element_type=jnp.float32)
        s_scr[c] = s
        m_c = _lane_fold(s, jnp.maximum)
        return m_c if m_run is None else jnp.maximum(m_run, m_c)

    def weigh_chunk(c, s_scr, m_cur, acc):
        s = s_scr[c]
        p = jnp.concatenate(
            [jnp.exp2(s[:, t * LANES:(t + 1) * LANES] - m_cur)
             for t in range(KEY_CHUNK // LANES)], axis=1).astype(jnp.bfloat16)
        pv = jnp.dot(p, vaug_scr[c * KEY_CHUNK:(c + 1) * KEY_CHUNK, :],
                     preferred_element_type=jnp.float32)
        return pv if acc is None else acc + pv

    def stage(g_cur, g_next, parity, defer_finish=False):
        s_cur, m_ref = bufs[parity]
        s_next, m_next = bufs[1 - parity]
        if g_next is not None:
            qs_next = make_qs(g_next)
        if g_cur is not None:
            m_cur = m_ref[...]
        m_run, acc = None, None
        for c in range(n_chunks):
            if g_next is not None:
                m_run = score_chunk(qs_next, c, s_next, m_run)
            if g_cur is not None:
                acc = weigh_chunk(c, s_cur, m_cur, acc)
        if g_next is not None:
            m = jnp.max(m_run, axis=-1, keepdims=True)
            m_next[...] = jnp.broadcast_to(m, (rows, LANES))
        if g_cur is not None:
            if defer_finish:
                acc_scr[...] = acc
            else:
                finish(g_cur, acc)

    stage(None, 0, 1)
    stage(0, 1, 0, defer_finish=True)

    def stages(g, count):
        finish(g, acc_scr[...])
        for u in range(1, count + 1):
            stage(g + u, g + u + 1, u % 2, defer_finish=(u == count))

    def body(j, carry):
        stages(GROUP_UNROLL * j, GROUP_UNROLL)
        return carry

    n_loop = (n_groups - 2) // GROUP_UNROLL
    lax.fori_loop(0, n_loop, body, 0)
    done = GROUP_UNROLL * n_loop
    if n_groups - 2 > done:
        stages(done, n_groups - 2 - done)
    finish(n_groups - 2, acc_scr[...])
    stage(n_groups - 1, None, 1)


def _row_start(g, bq):
    return g * bq if isinstance(g, int) else pl.multiple_of(g * bq, bq)


def _attn_a_kernel(lam_ref, subln_ref, q_ref, k_ref, v_ref, g_ref, o_ref,
                   vaug_scr, acc_scr, s0_scr, m0_scr, s1_scr, m1_scr, *, lam_init):
    bq = ATTN_ROWS // 2
    n_groups = q_ref.shape[0] // bq
    lam = lam_ref[...]
    lam_full = (jnp.exp(jnp.sum(lam[0:1] * lam[1:2], axis=-1, keepdims=True))
                - jnp.exp(jnp.sum(lam[2:3] * lam[3:4], axis=-1, keepdims=True))
                + lam_init)
    subln = subln_ref[...] * (1.0 - lam_init)

    def make_qs(g):
        q = q_ref[pl.ds(_row_start(g, bq), bq), :]
        lane = lax.broadcasted_iota(jnp.int32, q.shape, 1)
        zero = jnp.zeros_like(q)
        return jnp.concatenate([jnp.where(lane < A_QK_DIM, q, zero),
                                jnp.where(lane >= A_QK_DIM, q, zero)], axis=0)

    def finish(g, acc):
        r0 = _row_start(g, bq)
        o = acc[:, 0:LANES] / acc[:, LANES:2 * LANES]
        o = o[:bq] - lam_full * o[bq:]
        o = o * _rms_scale(o) * subln
        gate = g_ref[pl.ds(r0, bq), :].astype(jnp.float32)
        o_ref[pl.ds(r0, bq), :] = (o * gate).astype(o_ref.dtype)

    _attention_groups(make_qs, finish, n_groups, k_ref, v_ref, vaug_scr, acc_scr,
                      ((s0_scr, m0_scr), (s1_scr, m1_scr)))


def _attn_scratch(seq):
    n_chunks = seq // KEY_CHUNK
    scores = pltpu.VMEM((n_chunks, ATTN_ROWS, KEY_CHUNK), jnp.float32)
    maxima = pltpu.VMEM((ATTN_ROWS, LANES), jnp.float32)
    return [pltpu.VMEM((seq, 2 * LANES), jnp.bfloat16),
            pltpu.VMEM((ATTN_ROWS, 2 * LANES), jnp.float32),
            scores, maxima, scores, maxima]


def _attn_a(proj, batch, seq, lam, subln, lam_init):
    head_spec = lambda tile0: pl.BlockSpec((seq, LANES), lambda b, h: (b, tile0 + h))
    return pl.pallas_call(
        functools.partial(_attn_a_kernel, lam_init=lam_init),
        grid=(batch, A_HEADS),
        in_specs=[
            pl.BlockSpec((4, A_QK_DIM), lambda b, h: (0, 0)),
            pl.BlockSpec((1, LANES), lambda b, h: (0, 0)),
            head_spec(0), head_spec(A_HEADS), head_spec(2 * A_HEADS), head_spec(3 * A_HEADS),
        ],
        out_specs=head_spec(0),
        out_shape=jax.ShapeDtypeStruct((batch * seq, A_HEADS * LANES), jnp.bfloat16),
        scratch_shapes=_attn_scratch(seq),
        compiler_params=_compiler_params(2),
        name="attn_a",
    )(lam, subln.reshape(1, LANES), proj, proj, proj, proj)


def _attn_b_kernel(q_ref, k_ref, v_ref, g_ref, o_ref,
                   vaug_scr, acc_scr, s0_scr, m0_scr, s1_scr, m1_scr):
    bq = ATTN_ROWS // B_GROUP
    n_groups = q_ref.shape[0] // bq

    def make_qs(g):
        r0 = _row_start(g, bq)
        return jnp.concatenate(
            [q_ref[pl.ds(r0, bq), i * LANES:(i + 1) * LANES] for i in range(B_GROUP)], axis=0)

    def finish(g, acc):
        r0 = _row_start(g, bq)
        o = acc[:, 0:LANES] / acc[:, LANES:2 * LANES]
        for i in range(B_GROUP):
            gate = g_ref[pl.ds(r0, bq), i * LANES:(i + 1) * LANES].astype(jnp.float32)
            o_ref[pl.ds(r0, bq), i * LANES:(i + 1) * LANES] = (
                o[i * bq:(i + 1) * bq] * gate).astype(o_ref.dtype)

    _attention_groups(make_qs, finish, n_groups, k_ref, v_ref, vaug_scr, acc_scr,
                      ((s0_scr, m0_scr), (s1_scr, m1_scr)))


def _attn_b(proj, batch, seq):
    q_rows = min(seq, B_SPLIT_ROWS)
    n_split = seq // q_rows
    group_cols = B_GROUP * LANES
    k_tile0 = B_HEADS
    v_tile0 = B_HEADS + B_KV_HEADS
    g_blk0 = (B_HEADS + 2 * B_KV_HEADS) * LANES // group_cols
    q_spec = lambda blk0: pl.BlockSpec(
        (q_rows, group_cols), lambda b, kv, i: (b * n_split + i, blk0 + kv))
    kv_spec = lambda tile0: pl.BlockSpec((seq, LANES), lambda b, kv, i: (b, tile0 + kv))
    return pl.pallas_call(
        _attn_b_kernel,
        grid=(batch, B_KV_HEADS, n_split),
        in_specs=[q_spec(0), kv_spec(k_tile0), kv_spec(v_tile0), q_spec(g_blk0)],
        out_specs=q_spec(0),
        out_shape=jax.ShapeDtypeStruct((batch * seq, B_HEADS * LANES), jnp.bfloat16),
        scratch_shapes=_attn_scratch(seq),
        compiler_params=_compiler_params(3),
        name="attn_b",
    )(proj, proj, proj, proj)


def _outproj_rows(o_ref, w_ref, nw_ref, x_ref):
    m = jnp.dot(o_ref[...], w_ref[...], preferred_element_type=jnp.float32)
    return x_ref[...] + m * _rms_scale(m) * nw_ref[...]


def _outproj_kernel(o_ref, w_ref, nw_ref, x_ref, y_ref):
    y_ref[...] = _outproj_rows(o_ref, w_ref, nw_ref, x_ref)


def _out_in_kernel(o_ref, wo_ref, nwo_ref, x_ref, nwi_ref, wi_ref, c_ref, sa_ref, sb_ref,
                   qn_ref, kn_ref, y_ref, p_ref, *, mixer):
    y = _outproj_rows(o_ref, wo_ref, nwo_ref, x_ref)
    y_ref[...] = y
    _inproj_rows(y, nwi_ref, wi_ref, c_ref, sa_ref, sb_ref, qn_ref, kn_ref, p_ref, mixer)


def _out_in(o, w_out, norm_post, x2, seq, norm_pre, w_in, tables, qn, kn, mixer):
    n_tok = x2.shape[0]
    n_out = w_in.shape[1]
    rows = PROJ_ROWS[mixer]
    per_seq = seq // rows
    row_spec = pl.BlockSpec((rows, D_MODEL), lambda i: (i, 0))
    const = lambda shape: pl.BlockSpec(shape, lambda i: (0, 0))
    tab_spec = pl.BlockSpec((rows, LANES), lambda i: (i % per_seq, 0))
    return pl.pallas_call(
        functools.partial(_out_in_kernel, mixer=mixer),
        grid=(n_tok // rows,),
        in_specs=[
            row_spec, const((D_MODEL, D_MODEL)), const((1, D_MODEL)), row_spec,
            const((1, D_MODEL)), const((D_MODEL, n_out)),
            tab_spec, tab_spec, tab_spec, const((1, LANES)), const((1, LANES)),
        ],
        out_specs=[row_spec, pl.BlockSpec((rows, n_out), lambda i: (i, 0))],
        out_shape=[jax.ShapeDtypeStruct((n_tok, D_MODEL), jnp.float32),
                   jax.ShapeDtypeStruct((n_tok, n_out), jnp.bfloat16)],
        compiler_params=_compiler_params(1),
        name=f"out_in_{mixer}",
    )(o, w_out, norm_post.reshape(1, D_MODEL), x2, norm_pre.reshape(1, D_MODEL), w_in,
      *tables, qn.reshape(1, LANES), kn.reshape(1, LANES))


def _outproj(o, w_out, norm_w, x2):
    n_tok = x2.shape[0]
    row_spec = pl.BlockSpec((OUT_ROWS, D_MODEL), lambda i: (i, 0))
    return pl.pallas_call(
        _outproj_kernel,
        grid=(n_tok // OUT_ROWS,),
        in_specs=[
            row_spec,
            pl.BlockSpec((D_MODEL, D_MODEL), lambda i: (0, 0)),
            pl.BlockSpec((1, D_MODEL), lambda i: (0, 0)),
            row_spec,
        ],
        out_specs=row_spec,
        out_shape=jax.ShapeDtypeStruct((n_tok, D_MODEL), jnp.float32),
        compiler_params=_compiler_params(1),
        name="outproj",
    )(o, w_out, norm_w.reshape(1, D_MODEL), x2)


def _trunk(x, norm_pre, norm_post, a_w_in, a_w_out, a_lam, a_subln,
           b_w_in, b_w_out, b_q_norm, b_k_norm):
    batch, seq, _ = x.shape
    x2 = x.reshape(batch * seq, D_MODEL)
    tables_a = _rope_tables_a(seq)
    tables_b = _rope_tables_b(seq)
    ones = jnp.ones((LANES,), jnp.float32)

    def inproj_args(i):
        j = i // 2
        if i % 2 == 0:
            return (seq, norm_pre[i], a_w_in[j], tables_a, ones, ones, "a")
        return (seq, norm_pre[i], b_w_in[j], tables_b, b_q_norm[j], b_k_norm[j], "b")

    proj = _inproj(x2, *inproj_args(0))
    for i in range(DEPTH):
        j = i // 2
        if i % 2 == 0:
            lam_init = 0.8 - 0.6 * math.exp(-0.3 * i)
            o = _attn_a(proj, batch, seq, a_lam[j], a_subln[j], lam_init)
            w_out = a_w_out[j]
        else:
            o = _attn_b(proj, batch, seq)
            w_out = b_w_out[j]
        if i + 1 < DEPTH:
            x2, proj = _out_in(o, w_out, norm_post[i], x2, *inproj_args(i + 1))
        else:
            x2 = _outproj(o, w_out, norm_post[i], x2)
    return x2.reshape(batch, seq, D_MODEL)


def kernel(x_prompt, x_sample, norm_pre, norm_post, a_w_in, a_w_out, a_lam, a_subln,
           b_w_in, b_w_out, b_q_norm, b_k_norm):
    bf16 = jnp.bfloat16
    weights = (norm_pre, norm_post, a_w_in.astype(bf16), a_w_out.astype(bf16), a_lam,
               a_subln, b_w_in.astype(bf16), b_w_out.astype(bf16), b_q_norm, b_k_norm)
    return (_trunk(x_prompt, *weights), _trunk(x_sample, *weights))
```

```python
import functools
import math

import jax
import jax.numpy as jnp
from jax import lax
from jax.experimental import pallas as pl
from jax.experimental.pallas import tpu as pltpu

D_MODEL = 1024
DEPTH = 4
NORM_EPS = 1e-6

A_HEADS = 8
A_QK_DIM = 64
A_ROT_DIM = 16
A_IN = 4096
ROPE_THETA = 500000.0

B_HEADS = 8
B_KV_HEADS = 2
B_GROUP = B_HEADS // B_KV_HEADS
B_HEAD_DIM = 128
B_IN = 2560
B_AXIS_DIM = 64
AXIAL_THETA = 10000.0
GRID_W = 64

LANES = 128
LOG2E = 1.4426950408889634
VMEM_LIMIT_BYTES = 56 * 1024 * 1024

PROJ_ROWS = {"a": 512, "b": 512}
PROJ_SUB_ROWS = {"a": 512, "b": 256}
OUT_ROWS = 1024
PROJ_COLS = 512
ATTN_ROWS = 512
KEY_CHUNK = 512
GROUP_UNROLL = 4
B_SPLIT_ROWS = 2048


def _compiler_params(semantics):
    return pltpu.CompilerParams(
        dimension_semantics=semantics, vmem_limit_bytes=VMEM_LIMIT_BYTES)


def _angles(pos, dim, theta):
    inv_freq = theta ** (-(jnp.arange(0, dim, 2, dtype=jnp.float32) / dim))
    return pos[:, None] * inv_freq[None, :]


def _rope_tables_a(seq):
    ang = _angles(jnp.arange(seq, dtype=jnp.float32), A_ROT_DIM, ROPE_THETA)
    cos, sin = jnp.cos(ang), jnp.sin(ang)
    half = A_ROT_DIM // 2
    one = jnp.ones((seq, A_QK_DIM - A_ROT_DIM), jnp.float32)
    zero_h = jnp.zeros((seq, half), jnp.float32)
    zero_r = jnp.zeros((seq, A_QK_DIM - A_ROT_DIM), jnp.float32)
    c = jnp.concatenate([cos, cos, one], axis=1)
    sa = jnp.concatenate([-sin, zero_h, zero_r], axis=1)
    sb = jnp.concatenate([zero_h, sin, zero_r], axis=1)
    tile2 = lambda t: jnp.concatenate([t, t], axis=1)
    return tile2(c), tile2(sa), tile2(sb)


def _rope_tables_b(seq):
    rows = seq // GRID_W
    row_ids = jnp.repeat(jnp.arange(rows, dtype=jnp.float32), GRID_W)
    col_ids = jnp.tile(jnp.arange(GRID_W, dtype=jnp.float32), rows)
    ar = _angles(row_ids, B_AXIS_DIM, AXIAL_THETA)
    ac = _angles(col_ids, B_AXIS_DIM, AXIAL_THETA)
    z = jnp.zeros_like(ar)
    c = jnp.concatenate([jnp.cos(ar), jnp.cos(ar), jnp.cos(ac), jnp.cos(ac)], axis=1)
    sa = jnp.concatenate([-jnp.sin(ar), z, -jnp.sin(ac), z], axis=1)
    sb = jnp.concatenate([z, jnp.sin(ar), z, jnp.sin(ac)], axis=1)
    return c, sa, sb


def _rotate(y, c, sa, sb, shift):
    return (y * c + pltpu.roll(y, LANES - shift, axis=1) * sa
            + pltpu.roll(y, shift, axis=1) * sb)


def _silu(g):
    return g / (1.0 + jnp.exp(-g))


def _rms_scale(y):
    return lax.rsqrt(jnp.mean(y * y, axis=-1, keepdims=True) + NORM_EPS)


def _pre_norm(x, nw_ref):
    return (x * _rms_scale(x) * nw_ref[...]).astype(jnp.bfloat16)


def _inproj_kernel(x_ref, nw_ref, w_ref, c_ref, sa_ref, sb_ref, qn_ref, kn_ref,
                   o_ref, *, mixer):
    _inproj_rows(_pre_norm(x_ref[...], nw_ref), w_ref, c_ref, sa_ref, sb_ref,
                 qn_ref, kn_ref, o_ref, mixer)


def _inproj_rows(h, w_ref, c_ref, sa_ref, sb_ref, qn_ref, kn_ref, o_ref, mixer):
    sub = PROJ_SUB_ROWS[mixer]
    for r0 in range(0, h.shape[0], sub):
        rws = slice(r0, r0 + sub)
        _inproj_sub_rows(h[rws], w_ref, c_ref[rws, :], sa_ref[rws, :], sb_ref[rws, :],
                         qn_ref, kn_ref, o_ref, rws, mixer)


def _inproj_sub_rows(h, w_ref, c, sa, sb, qn_ref, kn_ref, o_ref, rws, mixer):
    n_out = o_ref.shape[1]
    if mixer == "a":
        n_q, n_k, shift = A_HEADS, A_HEADS, A_ROT_DIM // 2
        q_scale = (A_QK_DIM ** -0.5) * LOG2E
        n_plain = A_HEADS
    else:
        n_q, n_k, shift = B_HEADS, B_KV_HEADS, B_AXIS_DIM // 2
        q_scale = (B_HEAD_DIM ** -0.5) * LOG2E
        n_plain = B_KV_HEADS
    for col0 in range(0, n_out, PROJ_COLS):
        y = jnp.dot(h, w_ref[:, col0:col0 + PROJ_COLS],
                    preferred_element_type=jnp.float32)
        for t in range(PROJ_COLS // LANES):
            tile = col0 // LANES + t
            yt = y[:, t * LANES:(t + 1) * LANES]
            if tile < n_q + n_k:
                is_q = tile < n_q
                if mixer == "b":
                    yt = yt * _rms_scale(yt) * (qn_ref[...] if is_q else kn_ref[...])
                yt = _rotate(yt, c, sa, sb, shift)
                if is_q:
                    yt = yt * q_scale
            elif tile >= n_q + n_k + n_plain:
                yt = _silu(yt)
            o_ref[rws, tile * LANES:(tile + 1) * LANES] = yt.astype(o_ref.dtype)


def _inproj(x2, seq, norm_w, w_in, tables, qn, kn, mixer):
    n_tok = x2.shape[0]
    n_out = w_in.shape[1]
    rows = PROJ_ROWS[mixer]
    per_seq = seq // rows
    tab_spec = pl.BlockSpec((rows, LANES), lambda i: (i % per_seq, 0))
    vec_spec = pl.BlockSpec((1, LANES), lambda i: (0, 0))
    return pl.pallas_call(
        functools.partial(_inproj_kernel, mixer=mixer),
        grid=(n_tok // rows,),
        in_specs=[
            pl.BlockSpec((rows, D_MODEL), lambda i: (i, 0)),
            pl.BlockSpec((1, D_MODEL), lambda i: (0, 0)),
            pl.BlockSpec((D_MODEL, n_out), lambda i: (0, 0)),
            tab_spec, tab_spec, tab_spec, vec_spec, vec_spec,
        ],
        out_specs=pl.BlockSpec((rows, n_out), lambda i: (i, 0)),
        out_shape=jax.ShapeDtypeStruct((n_tok, n_out), jnp.bfloat16),
        compiler_params=_compiler_params(("parallel",)),
        name=f"inproj_{mixer}",
    )(x2, norm_w.reshape(1, D_MODEL), w_in, *tables,
      qn.reshape(1, LANES), kn.reshape(1, LANES))


def _lane_fold(x, op):
    out = x[:, 0:LANES]
    for t in range(1, x.shape[1] // LANES):
        out = op(out, x[:, t * LANES:(t + 1) * LANES])
    return out


def _attention_groups(make_qs, finish, n_groups, k_ref, v_ref, vaug_scr, acc_scr, bufs):
    n_chunks, rows, _ = bufs[0][0].shape
    vaug_scr[:, 0:LANES] = v_ref[...]
    vaug_scr[:, LANES:2 * LANES] = jnp.ones((v_ref.shape[0], LANES), vaug_scr.dtype)

    def score_chunk(qs, c, s_scr, m_run):
        s = lax.dot_general(qs, k_ref[c * KEY_CHUNK:(c + 1) * KEY_CHUNK, :],
                            (((1,), (1,)), ((), ())),
                            preferred_element_type=jnp.float32)
        s_scr[c] = s
        m_c = _lane_fold(s, jnp.maximum)
        return m_c if m_run is None else jnp.maximum(m_run, m_c)

    def weigh_chunk(c, s_scr, m_cur, acc):
        s = s_scr[c]
        p = jnp.concatenate(
            [jnp.exp2(s[:, t * LANES:(t + 1) * LANES] - m_cur)
             for t in range(KEY_CHUNK // LANES)], axis=1).astype(jnp.bfloat16)
        pv = jnp.dot(p, vaug_scr[c * KEY_CHUNK:(c + 1) * KEY_CHUNK, :],
                     preferred_element_type=jnp.float32)
        return pv if acc is None else acc + pv

    def stage(g_cur, g_next, parity, defer_finish=False):
        s_cur, m_ref = bufs[parity]
        s_next, m_next = bufs[1 - parity]
        if g_next is not None:
            qs_next = make_qs(g_next)
        if g_cur is not None:
            m_cur = m_ref[...]
        m_run, acc = None, None
        for c in range(n_chunks):
            if g_next is not None:
                m_run = score_chunk(qs_next, c, s_next, m_run)
            if g_cur is not None:
                acc = weigh_chunk(c, s_cur, m_cur, acc)
        if g_next is not None:
            m = jnp.max(m_run, axis=-1, keepdims=True)
            m_next[...] = jnp.broadcast_to(m, (rows, LANES))
        if g_cur is not None:
            if defer_finish:
                acc_scr[...] = acc
            else:
                finish(g_cur, acc)

    stage(None, 0, 1)
    stage(0, 1, 0, defer_finish=True)

    def stages(g, count):
        finish(g, acc_scr[...])
        for u in range(1, count + 1):
            stage(g + u, g + u + 1, u % 2, defer_finish=(u == count))

    def body(j, carry):
        stages(GROUP_UNROLL * j, GROUP_UNROLL)
        return carry

    n_loop = (n_groups - 2) // GROUP_UNROLL
    lax.fori_loop(0, n_loop, body, 0)
    done = GROUP_UNROLL * n_loop
    if n_groups - 2 > done:
        stages(done, n_groups - 2 - done)
    finish(n_groups - 2, acc_scr[...])
    stage(n_groups - 1, None, 1)


def _row_start(g, bq):
    return g * bq if isinstance(g, int) else pl.multiple_of(g * bq, bq)


def _attn_a_kernel(lam_ref, subln_ref, q_ref, k_ref, v_ref, g_ref, o_ref,
                   vaug_scr, acc_scr, s0_scr, m0_scr, s1_scr, m1_scr, *, lam_init):
    bq = ATTN_ROWS // 2
    n_groups = q_ref.shape[0] // bq
    lam = lam_ref[...]
    lam_full = (jnp.exp(jnp.sum(lam[0:1] * lam[1:2], axis=-1, keepdims=True))
                - jnp.exp(jnp.sum(lam[2:3] * lam[3:4], axis=-1, keepdims=True))
                + lam_init)
    subln = subln_ref[...] * (1.0 - lam_init)

    def make_qs(g):
        q = q_ref[pl.ds(_row_start(g, bq), bq), :]
        lane = lax.broadcasted_iota(jnp.int32, q.shape, 1)
        zero = jnp.zeros_like(q)
        return jnp.concatenate([jnp.where(lane < A_QK_DIM, q, zero),
                                jnp.where(lane >= A_QK_DIM, q, zero)], axis=0)

    def finish(g, acc):
        r0 = _row_start(g, bq)
        o = acc[:, 0:LANES] / acc[:, LANES:2 * LANES]
        o = o[:bq] - lam_full * o[bq:]
        o = o * _rms_scale(o) * subln
        gate = g_ref[pl.ds(r0, bq), :].astype(jnp.float32)
        o_ref[pl.ds(r0, bq), :] = (o * gate).astype(o_ref.dtype)

    _attention_groups(make_qs, finish, n_groups, k_ref, v_ref, vaug_scr, acc_scr,
                      ((s0_scr, m0_scr), (s1_scr, m1_scr)))


def _attn_scratch(seq):
    n_chunks = seq // KEY_CHUNK
    scores = pltpu.VMEM((n_chunks, ATTN_ROWS, KEY_CHUNK), jnp.float32)
    maxima = pltpu.VMEM((ATTN_ROWS, LANES), jnp.float32)
    return [pltpu.VMEM((seq, 2 * LANES), jnp.bfloat16),
            pltpu.VMEM((ATTN_ROWS, 2 * LANES), jnp.float32),
            scores, maxima, scores, maxima]


def _attn_a(proj, batch, seq, lam, subln, lam_init):
    head_spec = lambda tile0: pl.BlockSpec((seq, LANES), lambda b, h: (b, tile0 + h))
    return pl.pallas_call(
        functools.partial(_attn_a_kernel, lam_init=lam_init),
        grid=(batch, A_HEADS),
        in_specs=[
            pl.BlockSpec((4, A_QK_DIM), lambda b, h: (0, 0)),
            pl.BlockSpec((1, LANES), lambda b, h: (0, 0)),
            head_spec(0), head_spec(A_HEADS), head_spec(2 * A_HEADS), head_spec(3 * A_HEADS),
        ],
        out_specs=head_spec(0),
        out_shape=jax.ShapeDtypeStruct((batch * seq, A_HEADS * LANES), jnp.bfloat16),
        scratch_shapes=_attn_scratch(seq),
        compiler_params=_compiler_params(("parallel", "parallel")),
        name="attn_a",
    )(lam, subln.reshape(1, LANES), proj, proj, proj, proj)


def _attn_b_kernel(q_ref, k_ref, v_ref, g_ref, o_ref,
                   vaug_scr, acc_scr, s0_scr, m0_scr, s1_scr, m1_scr):
    bq = ATTN_ROWS // B_GROUP
    n_groups = q_ref.shape[0] // bq

    def make_qs(g):
        r0 = _row_start(g, bq)
        return jnp.concatenate(
            [q_ref[pl.ds(r0, bq), i * LANES:(i + 1) * LANES] for i in range(B_GROUP)], axis=0)

    def finish(g, acc):
        r0 = _row_start(g, bq)
        o = acc[:, 0:LANES] / acc[:, LANES:2 * LANES]
        for i in range(B_GROUP):
            gate = g_ref[pl.ds(r0, bq), i * LANES:(i + 1) * LANES].astype(jnp.float32)
            o_ref[pl.ds(r0, bq), i * LANES:(i + 1) * LANES] = (
                o[i * bq:(i + 1) * bq] * gate).astype(o_ref.dtype)

    _attention_groups(make_qs, finish, n_groups, k_ref, v_ref, vaug_scr, acc_scr,
                      ((s0_scr, m0_scr), (s1_scr, m1_scr)))


def _attn_b(proj, batch, seq):
    q_rows = min(seq, B_SPLIT_ROWS)
    n_split = seq // q_rows
    group_cols = B_GROUP * LANES
    k_tile0 = B_HEADS
    v_tile0 = B_HEADS + B_KV_HEADS
    g_blk0 = (B_HEADS + 2 * B_KV_HEADS) * LANES // group_cols
    q_spec = lambda blk0: pl.BlockSpec(
        (q_rows, group_cols), lambda b, kv, i: (b * n_split + i, blk0 + kv))
    kv_spec = lambda tile0: pl.BlockSpec((seq, LANES), lambda b, kv, i: (b, tile0 + kv))
    return pl.pallas_call(
        _attn_b_kernel,
        grid=(batch, B_KV_HEADS, n_split),
        in_specs=[q_spec(0), kv_spec(k_tile0), kv_spec(v_tile0), q_spec(g_blk0)],
        out_specs=q_spec(0),
        out_shape=jax.ShapeDtypeStruct((batch * seq, B_HEADS * LANES), jnp.bfloat16),
        scratch_shapes=_attn_scratch(seq),
        compiler_params=_compiler_params(("parallel", "parallel", "parallel")),
        name="attn_b",
    )(proj, proj, proj, proj)


def _outproj_rows(o_ref, w_ref, nw_ref, x_ref):
    m = jnp.dot(o_ref[...], w_ref[...], preferred_element_type=jnp.float32)
    return x_ref[...] + m * _rms_scale(m) * nw_ref[...]


def _outproj_kernel(o_ref, w_ref, nw_ref, x_ref, y_ref):
    y_ref[...] = _outproj_rows(o_ref, w_ref, nw_ref, x_ref)


def _out_in_kernel(o_ref, wo_ref, nwo_ref, x_ref, nwi_ref, wi_ref, c_ref, sa_ref, sb_ref,
                   qn_ref, kn_ref, y_ref, p_ref, h_scr, *, mixer):
    @pl.when(pl.program_id(0) == 0)
    def _():
        h_scr[...] = jnp.zeros_like(h_scr)

    h_prev = h_scr[...]
    y = _outproj_rows(o_ref, wo_ref, nwo_ref, x_ref)
    y_ref[...] = y
    _inproj_rows(h_prev, wi_ref, c_ref, sa_ref, sb_ref, qn_ref, kn_ref, p_ref, mixer)
    h_scr[...] = _pre_norm(y, nwi_ref)


def _out_in(o, w_out, norm_post, x2, seq, norm_pre, w_in, tables, qn, kn, mixer):
    n_tok = x2.shape[0]
    n_out = w_in.shape[1]
    rows = PROJ_ROWS[mixer]
    n_tiles = n_tok // rows
    per_seq = seq // rows
    cur = lambda i: (jnp.minimum(i, n_tiles - 1), 0)
    prev = lambda i: (jnp.maximum(i - 1, 0), 0)
    const = lambda shape: pl.BlockSpec(shape, lambda i: (0, 0))
    tab_spec = pl.BlockSpec((rows, LANES), lambda i: (jnp.maximum(i - 1, 0) % per_seq, 0))
    return pl.pallas_call(
        functools.partial(_out_in_kernel, mixer=mixer),
        grid=(n_tiles + 1,),
        in_specs=[
            pl.BlockSpec((rows, D_MODEL), cur), const((D_MODEL, D_MODEL)),
            const((1, D_MODEL)), pl.BlockSpec((rows, D_MODEL), cur),
            const((1, D_MODEL)), const((D_MODEL, n_out)),
            tab_spec, tab_spec, tab_spec, const((1, LANES)), const((1, LANES)),
        ],
        out_specs=[pl.BlockSpec((rows, D_MODEL), cur), pl.BlockSpec((rows, n_out), prev)],
        out_shape=[jax.ShapeDtypeStruct((n_tok, D_MODEL), jnp.float32),
                   jax.ShapeDtypeStruct((n_tok, n_out), jnp.bfloat16)],
        scratch_shapes=[pltpu.VMEM((rows, D_MODEL), jnp.bfloat16)],
        compiler_params=_compiler_params(("arbitrary",)),
        name=f"out_in_{mixer}",
    )(o, w_out, norm_post.reshape(1, D_MODEL), x2, norm_pre.reshape(1, D_MODEL), w_in,
      *tables, qn.reshape(1, LANES), kn.reshape(1, LANES))


def _outproj(o, w_out, norm_w, x2):
    n_tok = x2.shape[0]
    row_spec = pl.BlockSpec((OUT_ROWS, D_MODEL), lambda i: (i, 0))
    return pl.pallas_call(
        _outproj_kernel,
        grid=(n_tok // OUT_ROWS,),
        in_specs=[
            row_spec,
            pl.BlockSpec((D_MODEL, D_MODEL), lambda i: (0, 0)),
            pl.BlockSpec((1, D_MODEL), lambda i: (0, 0)),
            row_spec,
        ],
        out_specs=row_spec,
        out_shape=jax.ShapeDtypeStruct((n_tok, D_MODEL), jnp.float32),
        compiler_params=_compiler_params(("parallel",)),
        name="outproj",
    )(o, w_out, norm_w.reshape(1, D_MODEL), x2)


def _trunk(x, norm_pre, norm_post, a_w_in, a_w_out, a_lam, a_subln,
           b_w_in, b_w_out, b_q_norm, b_k_norm):
    batch, seq, _ = x.shape
    x2 = x.reshape(batch * seq, D_MODEL)
    tables_a = _rope_tables_a(seq)
    tables_b = _rope_tables_b(seq)
    ones = jnp.ones((LANES,), jnp.float32)

    def inproj_args(i):
        j = i // 2
        if i % 2 == 0:
            return (seq, norm_pre[i], a_w_in[j], tables_a, ones, ones, "a")
        return (seq, norm_pre[i], b_w_in[j], tables_b, b_q_norm[j], b_k_norm[j], "b")

    proj = _inproj(x2, *inproj_args(0))
    for i in range(DEPTH):
        j = i // 2
        if i % 2 == 0:
            lam_init = 0.8 - 0.6 * math.exp(-0.3 * i)
            o = _attn_a(proj, batch, seq, a_lam[j], a_subln[j], lam_init)
            w_out = a_w_out[j]
        else:
            o = _attn_b(proj, batch, seq)
            w_out = b_w_out[j]
        if i + 1 < DEPTH:
            x2, proj = _out_in(o, w_out, norm_post[i], x2, *inproj_args(i + 1))
        else:
            x2 = _outproj(o, w_out, norm_post[i], x2)
    return x2.reshape(batch, seq, D_MODEL)


def kernel(x_prompt, x_sample, norm_pre, norm_post, a_w_in, a_w_out, a_lam, a_subln,
           b_w_in, b_w_out, b_q_norm, b_k_norm):
    bf16 = jnp.bfloat16
    weights = (norm_pre, norm_post, a_w_in.astype(bf16), a_w_out.astype(bf16), a_lam,
               a_subln, b_w_in.astype(bf16), b_w_out.astype(bf16), b_q_norm, b_k_norm)
    return (_trunk(x_prompt, *weights), _trunk(x_sample, *weights))
```

```python
import functools
import math

import jax
import jax.numpy as jnp
from jax import lax
from jax.experimental import pallas as pl
from jax.experimental.pallas import tpu as pltpu

D_MODEL = 1024
DEPTH = 4
NORM_EPS = 1e-6

A_HEADS = 8
A_QK_DIM = 64
A_ROT_DIM = 16
A_IN = 4096
ROPE_THETA = 500000.0

B_HEADS = 8
B_KV_HEADS = 2
B_GROUP = B_HEADS // B_KV_HEADS
B_HEAD_DIM = 128
B_IN = 2560
B_AXIS_DIM = 64
AXIAL_THETA = 10000.0
GRID_W = 64

LANES = 128
LOG2E = 1.4426950408889634
VMEM_LIMIT_BYTES = 56 * 1024 * 1024

PROJ_ROWS = {"a": 512, "b": 512}
PROJ_SUB_ROWS = {"a": 512, "b": 256}
OUT_ROWS = 1024
PROJ_COLS = 512
ATTN_ROWS = 512
KEY_CHUNK = 512
GROUP_UNROLL = 14
B_SPLIT_ROWS = 2048


def _compiler_params(semantics):
    return pltpu.CompilerParams(
        dimension_semantics=semantics, vmem_limit_bytes=VMEM_LIMIT_BYTES)


def _angles(pos, dim, theta):
    inv_freq = theta ** (-(jnp.arange(0, dim, 2, dtype=jnp.float32) / dim))
    return pos[:, None] * inv_freq[None, :]


def _rope_tables_a(seq):
    ang = _angles(jnp.arange(seq, dtype=jnp.float32), A_ROT_DIM, ROPE_THETA)
    cos, sin = jnp.cos(ang), jnp.sin(ang)
    half = A_ROT_DIM // 2
    one = jnp.ones((seq, A_QK_DIM - A_ROT_DIM), jnp.float32)
    zero_h = jnp.zeros((seq, half), jnp.float32)
    zero_r = jnp.zeros((seq, A_QK_DIM - A_ROT_DIM), jnp.float32)
    c = jnp.concatenate([cos, cos, one], axis=1)
    sa = jnp.concatenate([-sin, zero_h, zero_r], axis=1)
    sb = jnp.concatenate([zero_h, sin, zero_r], axis=1)
    tile2 = lambda t: jnp.concatenate([t, t], axis=1)
    return tile2(c), tile2(sa), tile2(sb)


def _rope_tables_b(seq):
    rows = seq // GRID_W
    row_ids = jnp.repeat(jnp.arange(rows, dtype=jnp.float32), GRID_W)
    col_ids = jnp.tile(jnp.arange(GRID_W, dtype=jnp.float32), rows)
    ar = _angles(row_ids, B_AXIS_DIM, AXIAL_THETA)
    ac = _angles(col_ids, B_AXIS_DIM, AXIAL_THETA)
    z = jnp.zeros_like(ar)
    c = jnp.concatenate([jnp.cos(ar), jnp.cos(ar), jnp.cos(ac), jnp.cos(ac)], axis=1)
    sa = jnp.concatenate([-jnp.sin(ar), z, -jnp.sin(ac), z], axis=1)
    sb = jnp.concatenate([z, jnp.sin(ar), z, jnp.sin(ac)], axis=1)
    return c, sa, sb


def _rotate(y, c, sa, sb, shift):
    return (y * c + pltpu.roll(y, LANES - shift, axis=1) * sa
            + pltpu.roll(y, shift, axis=1) * sb)


def _silu(g):
    return g / (1.0 + jnp.exp(-g))


def _rms_scale(y):
    return lax.rsqrt(jnp.mean(y * y, axis=-1, keepdims=True) + NORM_EPS)


def _pre_norm(x, nw_ref):
    return (x * _rms_scale(x) * nw_ref[...]).astype(jnp.bfloat16)


def _inproj_kernel(x_ref, nw_ref, w_ref, c_ref, sa_ref, sb_ref, qn_ref, kn_ref,
                   o_ref, *, mixer):
    _inproj_rows(_pre_norm(x_ref[...], nw_ref), w_ref, c_ref, sa_ref, sb_ref,
                 qn_ref, kn_ref, o_ref, mixer)


def _inproj_rows(h, w_ref, c_ref, sa_ref, sb_ref, qn_ref, kn_ref, o_ref, mixer):
    sub = PROJ_SUB_ROWS[mixer]
    for r0 in range(0, h.shape[0], sub):
        rws = slice(r0, r0 + sub)
        _inproj_sub_rows(h[rws], w_ref, c_ref[rws, :], sa_ref[rws, :], sb_ref[rws, :],
                         qn_ref, kn_ref, o_ref, rws, mixer)


def _inproj_sub_rows(h, w_ref, c, sa, sb, qn_ref, kn_ref, o_ref, rws, mixer):
    n_out = o_ref.shape[1]
    if mixer == "a":
        n_q, n_k, shift = A_HEADS, A_HEADS, A_ROT_DIM // 2
        q_scale = (A_QK_DIM ** -0.5) * LOG2E
        n_plain = A_HEADS
    else:
        n_q, n_k, shift = B_HEADS, B_KV_HEADS, B_AXIS_DIM // 2
        q_scale = (B_HEAD_DIM ** -0.5) * LOG2E
        n_plain = B_KV_HEADS
    for col0 in range(0, n_out, PROJ_COLS):
        y = jnp.dot(h, w_ref[:, col0:col0 + PROJ_COLS],
                    preferred_element_type=jnp.float32)
        for t in range(PROJ_COLS // LANES):
            tile = col0 // LANES + t
            yt = y[:, t * LANES:(t + 1) * LANES]
            if tile < n_q + n_k:
                is_q = tile < n_q
                if mixer == "b":
                    yt = yt * _rms_scale(yt) * (qn_ref[...] if is_q else kn_ref[...])
                yt = _rotate(yt, c, sa, sb, shift)
                if is_q:
                    yt = yt * q_scale
            elif tile >= n_q + n_k + n_plain:
                yt = _silu(yt)
            o_ref[rws, tile * LANES:(tile + 1) * LANES] = yt.astype(o_ref.dtype)


def _inproj(x2, seq, norm_w, w_in, tables, qn, kn, mixer):
    n_tok = x2.shape[0]
    n_out = w_in.shape[1]
    rows = PROJ_ROWS[mixer]
    per_seq = seq // rows
    tab_spec = pl.BlockSpec((rows, LANES), lambda i: (i % per_seq, 0))
    vec_spec = pl.BlockSpec((1, LANES), lambda i: (0, 0))
    return pl.pallas_call(
        functools.partial(_inproj_kernel, mixer=mixer),
        grid=(n_tok // rows,),
        in_specs=[
            pl.BlockSpec((rows, D_MODEL), lambda i: (i, 0)),
            pl.BlockSpec((1, D_MODEL), lambda i: (0, 0)),
            pl.BlockSpec((D_MODEL, n_out), lambda i: (0, 0)),
            tab_spec, tab_spec, tab_spec, vec_spec, vec_spec,
        ],
        out_specs=pl.BlockSpec((rows, n_out), lambda i: (i, 0)),
        out_shape=jax.ShapeDtypeStruct((n_tok, n_out), jnp.bfloat16),
        compiler_params=_compiler_params(("parallel",)),
        name=f"inproj_{mixer}",
    )(x2, norm_w.reshape(1, D_MODEL), w_in, *tables,
      qn.reshape(1, LANES), kn.reshape(1, LANES))


def _lane_fold(x, op):
    out = x[:, 0:LANES]
    for t in range(1, x.shape[1] // LANES):
        out = op(out, x[:, t * LANES:(t + 1) * LANES])
    return out


def _attention_groups(make_qs, finish, n_groups, k_ref, v_ref, vaug_scr, acc_scr, bufs):
    n_chunks, rows, _ = bufs[0][0].shape
    vaug_scr[:, 0:LANES] = v_ref[...]
    vaug_scr[:, LANES:2 * LANES] = jnp.ones((v_ref.shape[0], LANES), vaug_scr.dtype)

    def score_chunk(qs, c, s_scr, m_run):
        s = lax.dot_general(qs, k_ref[c * KEY_CHUNK:(c + 1) * KEY_CHUNK, :],
                            (((1,), (1,)), ((), ())),
                            preferred_element_type=jnp.float32)
        s_scr[c] = s
        m_c = _lane_fold(s, jnp.maximum)
        return m_c if m_run is None else jnp.maximum(m_run, m_c)

    def weigh_chunk(c, s_scr, m_cur, acc):
        s = s_scr[c]
        p = jnp.concatenate(
            [jnp.exp2(s[:, t * LANES:(t + 1) * LANES] - m_cur)
             for t in range(KEY_CHUNK // LANES)], axis=1).astype(jnp.bfloat16)
        pv = jnp.dot(p, vaug_scr[c * KEY_CHUNK:(c + 1) * KEY_CHUNK, :],
                     preferred_element_type=jnp.float32)
        return pv if acc is None else acc + pv

    def stage(g_cur, g_next, parity, defer_finish=False):
        s_cur, m_ref = bufs[parity]
        s_next, m_next = bufs[1 - parity]
        if g_next is not None:
            qs_next = make_qs(g_next)
        if g_cur is not None:
            m_cur = m_ref[...]
        m_run, acc = None, None
        for c in range(n_chunks):
            if g_next is not None:
                m_run = score_chunk(qs_next, c, s_next, m_run)
            if g_cur is not None:
                acc = weigh_chunk(c, s_cur, m_cur, acc)
        if g_next is not None:
            m = jnp.max(m_run, axis=-1, keepdims=True)
            m_next[...] = jnp.broadcast_to(m, (rows, LANES))
        if g_cur is not None:
            if defer_finish:
                acc_scr[...] = acc
            else:
                finish(g_cur, acc)

    stage(None, 0, 1)
    stage(0, 1, 0, defer_finish=True)

    def stages(g, count):
        finish(g, acc_scr[...])
        for u in range(1, count + 1):
            stage(g + u, g + u + 1, u % 2, defer_finish=(u == count))

    def body(j, carry):
        stages(GROUP_UNROLL * j, GROUP_UNROLL)
        return carry

    n_loop = (n_groups - 2) // GROUP_UNROLL
    lax.fori_loop(0, n_loop, body, 0)
    done = GROUP_UNROLL * n_loop
    if n_groups - 2 > done:
        stages(done, n_groups - 2 - done)
    finish(n_groups - 2, acc_scr[...])
    stage(n_groups - 1, None, 1)


def _row_start(g, bq):
    return g * bq if isinstance(g, int) else pl.multiple_of(g * bq, bq)


def _attn_a_kernel(lam_ref, subln_ref, q_ref, k_ref, v_ref, g_ref, o_ref,
                   vaug_scr, acc_scr, s0_scr, m0_scr, s1_scr, m1_scr, *, lam_init):
    bq = ATTN_ROWS // 2
    n_groups = q_ref.shape[0] // bq
    lam = lam_ref[...]
    lam_full = (jnp.exp(jnp.sum(lam[0:1] * lam[1:2], axis=-1, keepdims=True))
                - jnp.exp(jnp.sum(lam[2:3] * lam[3:4], axis=-1, keepdims=True))
                + lam_init)
    subln = subln_ref[...] * (1.0 - lam_init)

    def make_qs(g):
        q = q_ref[pl.ds(_row_start(g, bq), bq), :]
        lane = lax.broadcasted_iota(jnp.int32, q.shape, 1)
        zero = jnp.zeros_like(q)
        return jnp.concatenate([jnp.where(lane < A_QK_DIM, q, zero),
                                jnp.where(lane >= A_QK_DIM, q, zero)], axis=0)

    def finish(g, acc):
        r0 = _row_start(g, bq)
        o = acc[:, 0:LANES] / acc[:, LANES:2 * LANES]
        o = o[:bq] - lam_full * o[bq:]
        o = o * _rms_scale(o) * subln
        gate = g_ref[pl.ds(r0, bq), :].astype(jnp.float32)
        o_ref[pl.ds(r0, bq), :] = (o * gate).astype(o_ref.dtype)

    _attention_groups(make_qs, finish, n_groups, k_ref, v_ref, vaug_scr, acc_scr,
                      ((s0_scr, m0_scr), (s1_scr, m1_scr)))


def _attn_scratch(seq):
    n_chunks = seq // KEY_CHUNK
    scores = pltpu.VMEM((n_chunks, ATTN_ROWS, KEY_CHUNK), jnp.float32)
    maxima = pltpu.VMEM((ATTN_ROWS, LANES), jnp.float32)
    return [pltpu.VMEM((seq, 2 * LANES), jnp.bfloat16),
            pltpu.VMEM((ATTN_ROWS, 2 * LANES), jnp.float32),
            scores, maxima, scores, maxima]


def _attn_a(proj, batch, seq, lam, subln, lam_init):
    head_spec = lambda tile0: pl.BlockSpec((seq, LANES), lambda b, h: (b, tile0 + h))
    return pl.pallas_call(
        functools.partial(_attn_a_kernel, lam_init=lam_init),
        grid=(batch, A_HEADS),
        in_specs=[
            pl.BlockSpec((4, A_QK_DIM), lambda b, h: (0, 0)),
            pl.BlockSpec((1, LANES), lambda b, h: (0, 0)),
            head_spec(0), head_spec(A_HEADS), head_spec(2 * A_HEADS), head_spec(3 * A_HEADS),
        ],
        out_specs=head_spec(0),
        out_shape=jax.ShapeDtypeStruct((batch * seq, A_HEADS * LANES), jnp.bfloat16),
        scratch_shapes=_attn_scratch(seq),
        compiler_params=_compiler_params(("parallel", "parallel")),
        name="attn_a",
    )(lam, subln.reshape(1, LANES), proj, proj, proj, proj)


def _attn_b_kernel(q_ref, k_ref, v_ref, g_ref, o_ref,
                   vaug_scr, acc_scr, s0_scr, m0_scr, s1_scr, m1_scr):
    bq = ATTN_ROWS // B_GROUP
    n_groups = q_ref.shape[0] // bq

    def make_qs(g):
        r0 = _row_start(g, bq)
        return jnp.concatenate(
            [q_ref[pl.ds(r0, bq), i * LANES:(i + 1) * LANES] for i in range(B_GROUP)], axis=0)

    def finish(g, acc):
        r0 = _row_start(g, bq)
        o = acc[:, 0:LANES] / acc[:, LANES:2 * LANES]
        for i in range(B_GROUP):
            gate = g_ref[pl.ds(r0, bq), i * LANES:(i + 1) * LANES].astype(jnp.float32)
            o_ref[pl.ds(r0, bq), i * LANES:(i + 1) * LANES] = (
                o[i * bq:(i + 1) * bq] * gate).astype(o_ref.dtype)

    _attention_groups(make_qs, finish, n_groups, k_ref, v_ref, vaug_scr, acc_scr,
                      ((s0_scr, m0_scr), (s1_scr, m1_scr)))


def _attn_b(proj, batch, seq):
    q_rows = min(seq, B_SPLIT_ROWS)
    n_split = seq // q_rows
    group_cols = B_GROUP * LANES
    k_tile0 = B_HEADS
    v_tile0 = B_HEADS + B_KV_HEADS
    g_blk0 = (B_HEADS + 2 * B_KV_HEADS) * LANES // group_cols
    q_spec = lambda blk0: pl.BlockSpec(
        (q_rows, group_cols), lambda b, kv, i: (b * n_split + i, blk0 + kv))
    kv_spec = lambda tile0: pl.BlockSpec((seq, LANES), lambda b, kv, i: (b, tile0 + kv))
    return pl.pallas_call(
        _attn_b_kernel,
        grid=(batch, B_KV_HEADS, n_split),
        in_specs=[q_spec(0), kv_spec(k_tile0), kv_spec(v_tile0), q_spec(g_blk0)],
        out_specs=q_spec(0),
        out_shape=jax.ShapeDtypeStruct((batch * seq, B_HEADS * LANES), jnp.bfloat16),
        scratch_shapes=_attn_scratch(seq),
        compiler_params=_compiler_params(("parallel", "parallel", "parallel")),
        name="attn_b",
    )(proj, proj, proj, proj)


def _outproj_rows(o_ref, w_ref, nw_ref, x_ref):
    m = jnp.dot(o_ref[...], w_ref[...], preferred_element_type=jnp.float32)
    return x_ref[...] + m * _rms_scale(m) * nw_ref[...]


def _outproj_kernel(o_ref, w_ref, nw_ref, x_ref, y_ref):
    y_ref[...] = _outproj_rows(o_ref, w_ref, nw_ref, x_ref)


def _out_in_kernel(o_ref, wo_ref, nwo_ref, x_ref, nwi_ref, wi_ref, c_ref, sa_ref, sb_ref,
                   qn_ref, kn_ref, y_ref, p_ref, h_scr, *, mixer):
    @pl.when(pl.program_id(0) == 0)
    def _():
        h_scr[...] = jnp.zeros_like(h_scr)

    h_prev = h_scr[...]
    y = _outproj_rows(o_ref, wo_ref, nwo_ref, x_ref)
    y_ref[...] = y
    _inproj_rows(h_prev, wi_ref, c_ref, sa_ref, sb_ref, qn_ref, kn_ref, p_ref, mixer)
    h_scr[...] = _pre_norm(y, nwi_ref)


def _out_in(o, w_out, norm_post, x2, seq, norm_pre, w_in, tables, qn, kn, mixer):
    n_tok = x2.shape[0]
    n_out = w_in.shape[1]
    rows = PROJ_ROWS[mixer]
    n_tiles = n_tok // rows
    per_seq = seq // rows
    cur = lambda i: (jnp.minimum(i, n_tiles - 1), 0)
    prev = lambda i: (jnp.maximum(i - 1, 0), 0)
    const = lambda shape: pl.BlockSpec(shape, lambda i: (0, 0))
    tab_spec = pl.BlockSpec((rows, LANES), lambda i: (jnp.maximum(i - 1, 0) % per_seq, 0))
    return pl.pallas_call(
        functools.partial(_out_in_kernel, mixer=mixer),
        grid=(n_tiles + 1,),
        in_specs=[
            pl.BlockSpec((rows, D_MODEL), cur), const((D_MODEL, D_MODEL)),
            const((1, D_MODEL)), pl.BlockSpec((rows, D_MODEL), cur),
            const((1, D_MODEL)), const((D_MODEL, n_out)),
            tab_spec, tab_spec, tab_spec, const((1, LANES)), const((1, LANES)),
        ],
        out_specs=[pl.BlockSpec((rows, D_MODEL), cur), pl.BlockSpec((rows, n_out), prev)],
        out_shape=[jax.ShapeDtypeStruct((n_tok, D_MODEL), jnp.float32),
                   jax.ShapeDtypeStruct((n_tok, n_out), jnp.bfloat16)],
        scratch_shapes=[pltpu.VMEM((rows, D_MODEL), jnp.bfloat16)],
        compiler_params=_compiler_params(("arbitrary",)),
        name=f"out_in_{mixer}",
    )(o, w_out, norm_post.reshape(1, D_MODEL), x2, norm_pre.reshape(1, D_MODEL), w_in,
      *tables, qn.reshape(1, LANES), kn.reshape(1, LANES))


def _outproj(o, w_out, norm_w, x2):
    n_tok = x2.shape[0]
    row_spec = pl.BlockSpec((OUT_ROWS, D_MODEL), lambda i: (i, 0))
    return pl.pallas_call(
        _outproj_kernel,
        grid=(n_tok // OUT_ROWS,),
        in_specs=[
            row_spec,
            pl.BlockSpec((D_MODEL, D_MODEL), lambda i: (0, 0)),
            pl.BlockSpec((1, D_MODEL), lambda i: (0, 0)),
            row_spec,
        ],
        out_specs=row_spec,
        out_shape=jax.ShapeDtypeStruct((n_tok, D_MODEL), jnp.float32),
        compiler_params=_compiler_params(("parallel",)),
        name="outproj",
    )(o, w_out, norm_w.reshape(1, D_MODEL), x2)


def _trunk(x, norm_pre, norm_post, a_w_in, a_w_out, a_lam, a_subln,
           b_w_in, b_w_out, b_q_norm, b_k_norm):
    batch, seq, _ = x.shape
    x2 = x.reshape(batch * seq, D_MODEL)
    tables_a = _rope_tables_a(seq)
    tables_b = _rope_tables_b(seq)
    ones = jnp.ones((LANES,), jnp.float32)

    def inproj_args(i):
        j = i // 2
        if i % 2 == 0:
            return (seq, norm_pre[i], a_w_in[j], tables_a, ones, ones, "a")
        return (seq, norm_pre[i], b_w_in[j], tables_b, b_q_norm[j], b_k_norm[j], "b")

    proj = _inproj(x2, *inproj_args(0))
    for i in range(DEPTH):
        j = i // 2
        if i % 2 == 0:
            lam_init = 0.8 - 0.6 * math.exp(-0.3 * i)
            o = _attn_a(proj, batch, seq, a_lam[j], a_subln[j], lam_init)
            w_out = a_w_out[j]
        else:
            o = _attn_b(proj, batch, seq)
            w_out = b_w_out[j]
        if i + 1 < DEPTH:
            x2, proj = _out_in(o, w_out, norm_post[i], x2, *inproj_args(i + 1))
        else:
            x2 = _outproj(o, w_out, norm_post[i], x2)
    return x2.reshape(batch, seq, D_MODEL)


def kernel(x_prompt, x_sample, norm_pre, norm_post, a_w_in, a_w_out, a_lam, a_subln,
           b_w_in, b_w_out, b_q_norm, b_k_norm):
    bf16 = jnp.bfloat16
    weights = (norm_pre, norm_post, a_w_in.astype(bf16), a_w_out.astype(bf16), a_lam,
               a_subln, b_w_in.astype(bf16), b_w_out.astype(bf16), b_q_norm, b_k_norm)
    return (_trunk(x_prompt, *weights), _trunk(x_sample, *weights))
```

```python
import functools
import math

import jax
import jax.numpy as jnp
from jax import lax
from jax.experimental import pallas as pl
from jax.experimental.pallas import tpu as pltpu

D_MODEL = 1024
DEPTH = 4
NORM_EPS = 1e-6

A_HEADS = 8
A_QK_DIM = 64
A_ROT_DIM = 16
ROPE_THETA = 500000.0

B_HEADS = 8
B_KV_HEADS = 2
B_GROUP = B_HEADS // B_KV_HEADS
B_HEAD_DIM = 128
B_AXIS_DIM = 64
AXIAL_THETA = 10000.0
GRID_W = 64

LANES = 128
LOG2E = 1.4426950408889634
VMEM_LIMIT_BYTES = 56 * 1024 * 1024

PROJ_ROWS = {"a": 512, "b": 512}
PROJ_SUB_ROWS = {"a": 512, "b": 256}
FIRST_PROJ_ROWS = 1024
OUT_ROWS = 1024
PROJ_COLS = 512
ATTN_ROWS = 512
KEY_CHUNK = 512
GROUP_UNROLL = 4
B_SPLIT_ROWS = 2048


def _compiler_params(semantics):
    return pltpu.CompilerParams(
        dimension_semantics=semantics, vmem_limit_bytes=VMEM_LIMIT_BYTES)


def _angles(pos, dim, theta):
    inv_freq = theta ** (-(jnp.arange(0, dim, 2, dtype=jnp.float32) / dim))
    return pos[:, None] * inv_freq[None, :]


def _rope_tables_a(seq):
    ang = _angles(jnp.arange(seq, dtype=jnp.float32), A_ROT_DIM, ROPE_THETA)
    cos, sin = jnp.cos(ang), jnp.sin(ang)
    half = A_ROT_DIM // 2
    one = jnp.ones((seq, A_QK_DIM - A_ROT_DIM), jnp.float32)
    zero_h = jnp.zeros((seq, half), jnp.float32)
    zero_r = jnp.zeros((seq, A_QK_DIM - A_ROT_DIM), jnp.float32)
    c = jnp.concatenate([cos, cos, one], axis=1)
    sa = jnp.concatenate([-sin, zero_h, zero_r], axis=1)
    sb = jnp.concatenate([zero_h, sin, zero_r], axis=1)
    tile2 = lambda t: jnp.concatenate([t, t], axis=1)
    return tile2(c), tile2(sa), tile2(sb)


def _rope_tables_b(seq):
    rows = seq // GRID_W
    row_ids = jnp.repeat(jnp.arange(rows, dtype=jnp.float32), GRID_W)
    col_ids = jnp.tile(jnp.arange(GRID_W, dtype=jnp.float32), rows)
    ar = _angles(row_ids, B_AXIS_DIM, AXIAL_THETA)
    ac = _angles(col_ids, B_AXIS_DIM, AXIAL_THETA)
    z = jnp.zeros_like(ar)
    c = jnp.concatenate([jnp.cos(ar), jnp.cos(ar), jnp.cos(ac), jnp.cos(ac)], axis=1)
    sa = jnp.concatenate([-jnp.sin(ar), z, -jnp.sin(ac), z], axis=1)
    sb = jnp.concatenate([z, jnp.sin(ar), z, jnp.sin(ac)], axis=1)
    return c, sa, sb


def _rotate(y, c, sa, sb, shift):
    return (y * c + pltpu.roll(y, LANES - shift, axis=1) * sa
            + pltpu.roll(y, shift, axis=1) * sb)


def _silu(g):
    return g / (1.0 + jnp.exp(-g))


def _rms_scale(y):
    return lax.rsqrt(jnp.mean(y * y, axis=-1, keepdims=True) + NORM_EPS)


def _pre_norm(x, nw_ref):
    return (x * _rms_scale(x) * nw_ref[...]).astype(jnp.bfloat16)


def _inproj_kernel(x_ref, nw_ref, w_ref, c_ref, sa_ref, sb_ref, qn_ref, kn_ref,
                   o_ref, *, mixer):
    _inproj_rows(_pre_norm(x_ref[...], nw_ref), w_ref, c_ref, sa_ref, sb_ref,
                 qn_ref, kn_ref, o_ref, mixer)


def _inproj_rows(h, w_ref, c_ref, sa_ref, sb_ref, qn_ref, kn_ref, o_ref, mixer):
    sub = PROJ_SUB_ROWS[mixer]
    for r0 in range(0, h.shape[0], sub):
        rws = slice(r0, r0 + sub)
        _inproj_sub_rows(h[rws], w_ref, c_ref[rws, :], sa_ref[rws, :], sb_ref[rws, :],
                         qn_ref, kn_ref, o_ref, rws, mixer)


def _inproj_sub_rows(h, w_ref, c, sa, sb, qn_ref, kn_ref, o_ref, rws, mixer):
    n_out = o_ref.shape[1]
    if mixer == "a":
        n_q, n_k, shift = A_HEADS, A_HEADS, A_ROT_DIM // 2
        q_scale = (A_QK_DIM ** -0.5) * LOG2E
        n_plain = A_HEADS
    else:
        n_q, n_k, shift = B_HEADS, B_KV_HEADS, B_AXIS_DIM // 2
        q_scale = (B_HEAD_DIM ** -0.5) * LOG2E
        n_plain = B_KV_HEADS
    for col0 in range(0, n_out, PROJ_COLS):
        y = jnp.dot(h, w_ref[:, col0:col0 + PROJ_COLS],
                    preferred_element_type=jnp.float32)
        for t in range(PROJ_COLS // LANES):
            tile = col0 // LANES + t
            yt = y[:, t * LANES:(t + 1) * LANES]
            if tile < n_q + n_k:
                is_q = tile < n_q
                if mixer == "b":
                    yt = yt * _rms_scale(yt) * (qn_ref[...] if is_q else kn_ref[...])
                yt = _rotate(yt, c, sa, sb, shift)
                if is_q:
                    yt = yt * q_scale
            elif tile >= n_q + n_k + n_plain:
                yt = _silu(yt)
            o_ref[rws, tile * LANES:(tile + 1) * LANES] = yt.astype(o_ref.dtype)


def _inproj(x2, seq, norm_w, w_in, tables, qn, kn, mixer):
    n_tok = x2.shape[0]
    n_out = w_in.shape[1]
    rows = FIRST_PROJ_ROWS
    per_seq = seq // rows
    tab_spec = pl.BlockSpec((rows, LANES), lambda i: (i % per_seq, 0))
    vec_spec = pl.BlockSpec((1, LANES), lambda i: (0, 0))
    return pl.pallas_call(
        functools.partial(_inproj_kernel, mixer=mixer),
        grid=(n_tok // rows,),
        in_specs=[
            pl.BlockSpec((rows, D_MODEL), lambda i: (i, 0)),
            pl.BlockSpec((1, D_MODEL), lambda i: (0, 0)),
            pl.BlockSpec((D_MODEL, n_out), lambda i: (0, 0)),
            tab_spec, tab_spec, tab_spec, vec_spec, vec_spec,
        ],
        out_specs=pl.BlockSpec((rows, n_out), lambda i: (i, 0)),
        out_shape=jax.ShapeDtypeStruct((n_tok, n_out), jnp.bfloat16),
        compiler_params=_compiler_params(("parallel",)),
        name=f"inproj_{mixer}",
    )(x2, norm_w.reshape(1, D_MODEL), w_in, *tables,
      qn.reshape(1, LANES), kn.reshape(1, LANES))


def _lane_fold(x, op):
    out = x[:, 0:LANES]
    for t in range(1, x.shape[1] // LANES):
        out = op(out, x[:, t * LANES:(t + 1) * LANES])
    return out


def _attention_groups(make_qs, finish, n_groups, k_ref, v_ref, vaug_scr, acc_scr, bufs):
    n_chunks, rows, _ = bufs[0][0].shape
    vaug_scr[:, 0:LANES] = v_ref[...]
    vaug_scr[:, LANES:2 * LANES] = jnp.ones((v_ref.shape[0], LANES), vaug_scr.dtype)

    def score_chunk(qs, c, s_scr, m_run):
        s = lax.dot_general(qs, k_ref[c * KEY_CHUNK:(c + 1) * KEY_CHUNK, :],
                            (((1,), (1,)), ((), ())),
                            preferred_element_type=jnp.float32)
        s_scr[c] = s
        m_c = _lane_fold(s, jnp.maximum)
        return m_c if m_run is None else jnp.maximum(m_run, m_c)

    def weigh_chunk(c, s_scr, m_cur, acc):
        s = s_scr[c]
        p = jnp.concatenate(
            [jnp.exp2(s[:, t * LANES:(t + 1) * LANES] - m_cur)
             for t in range(KEY_CHUNK // LANES)], axis=1).astype(jnp.bfloat16)
        pv = jnp.dot(p, vaug_scr[c * KEY_CHUNK:(c + 1) * KEY_CHUNK, :],
                     preferred_element_type=jnp.float32)
        return pv if acc is None else acc + pv

    def stage(g_cur, g_next, parity, defer_finish=False):
        s_cur, m_ref = bufs[parity]
        s_next, m_next = bufs[1 - parity]
        if g_next is not None:
            qs_next = make_qs(g_next)
        if g_cur is not None:
            m_cur = m_ref[...]
        m_run, acc = None, None
        for c in range(n_chunks):
            if g_next is not None:
                m_run = score_chunk(qs_next, c, s_next, m_run)
            if g_cur is not None:
                acc = weigh_chunk(c, s_cur, m_cur, acc)
        if g_next is not None:
            m = jnp.max(m_run, axis=-1, keepdims=True)
            m_next[...] = jnp.broadcast_to(m, (rows, LANES))
        if g_cur is not None:
            if defer_finish:
                acc_scr[...] = acc
            else:
                finish(g_cur, acc)

    stage(None, 0, 1)
    stage(0, 1, 0, defer_finish=True)

    def stages(g, count):
        finish(g, acc_scr[...])
        for u in range(1, count + 1):
            stage(g + u, g + u + 1, u % 2, defer_finish=(u == count))

    def body(j, carry):
        stages(GROUP_UNROLL * j, GROUP_UNROLL)
        return carry

    n_loop = (n_groups - 2) // GROUP_UNROLL
    lax.fori_loop(0, n_loop, body, 0)
    done = GROUP_UNROLL * n_loop
    if n_groups - 2 > done:
        stages(done, n_groups - 2 - done)
    finish(n_groups - 2, acc_scr[...])
    stage(n_groups - 1, None, 1)


def _row_start(g, bq):
    return g * bq if isinstance(g, int) else pl.multiple_of(g * bq, bq)


def _attn_a_kernel(lam_ref, subln_ref, q_ref, k_ref, v_ref, g_ref, o_ref,
                   vaug_scr, acc_scr, s0_scr, m0_scr, s1_scr, m1_scr, *, lam_init):
    bq = ATTN_ROWS // 2
    n_groups = q_ref.shape[0] // bq
    lam = lam_ref[...]
    lam_full = (jnp.exp(jnp.sum(lam[0:1] * lam[1:2], axis=-1, keepdims=True))
                - jnp.exp(jnp.sum(lam[2:3] * lam[3:4], axis=-1, keepdims=True))
                + lam_init)
    subln = subln_ref[...] * (1.0 - lam_init)

    def make_qs(g):
        q = q_ref[pl.ds(_row_start(g, bq), bq), :]
        lane = lax.broadcasted_iota(jnp.int32, q.shape, 1)
        zero = jnp.zeros_like(q)
        return jnp.concatenate([jnp.where(lane < A_QK_DIM, q, zero),
                                jnp.where(lane >= A_QK_DIM, q, zero)], axis=0)

    def finish(g, acc):
        r0 = _row_start(g, bq)
        o = acc[:, 0:LANES] / acc[:, LANES:2 * LANES]
        o = o[:bq] - lam_full * o[bq:]
        o = o * _rms_scale(o) * subln
        gate = g_ref[pl.ds(r0, bq), :].astype(jnp.float32)
        o_ref[pl.ds(r0, bq), :] = (o * gate).astype(o_ref.dtype)

    _attention_groups(make_qs, finish, n_groups, k_ref, v_ref, vaug_scr, acc_scr,
                      ((s0_scr, m0_scr), (s1_scr, m1_scr)))


def _attn_scratch(seq):
    n_chunks = seq // KEY_CHUNK
    scores = pltpu.VMEM((n_chunks, ATTN_ROWS, KEY_CHUNK), jnp.float32)
    maxima = pltpu.VMEM((ATTN_ROWS, LANES), jnp.float32)
    return [pltpu.VMEM((seq, 2 * LANES), jnp.bfloat16),
            pltpu.VMEM((ATTN_ROWS, 2 * LANES), jnp.float32),
            scores, maxima, scores, maxima]


def _attn_a(proj, batch, seq, lam, subln, lam_init):
    head_spec = lambda tile0: pl.BlockSpec((seq, LANES), lambda b, h: (b, tile0 + h))
    return pl.pallas_call(
        functools.partial(_attn_a_kernel, lam_init=lam_init),
        grid=(batch, A_HEADS),
        in_specs=[
            pl.BlockSpec((4, A_QK_DIM), lambda b, h: (0, 0)),
            pl.BlockSpec((1, LANES), lambda b, h: (0, 0)),
            head_spec(0), head_spec(A_HEADS), head_spec(2 * A_HEADS), head_spec(3 * A_HEADS),
        ],
        out_specs=head_spec(0),
        out_shape=jax.ShapeDtypeStruct((batch * seq, A_HEADS * LANES), jnp.bfloat16),
        scratch_shapes=_attn_scratch(seq),
        compiler_params=_compiler_params(("parallel", "parallel")),
        name="attn_a",
    )(lam, subln.reshape(1, LANES), proj, proj, proj, proj)


def _attn_b_kernel(q_ref, k_ref, v_ref, g_ref, o_ref,
                   vaug_scr, acc_scr, s0_scr, m0_scr, s1_scr, m1_scr):
    bq = ATTN_ROWS // B_GROUP
    n_groups = q_ref.shape[0] // bq

    def make_qs(g):
        r0 = _row_start(g, bq)
        return jnp.concatenate(
            [q_ref[pl.ds(r0, bq), i * LANES:(i + 1) * LANES] for i in range(B_GROUP)], axis=0)

    def finish(g, acc):
        r0 = _row_start(g, bq)
        o = acc[:, 0:LANES] / acc[:, LANES:2 * LANES]
        for i in range(B_GROUP):
            gate = g_ref[pl.ds(r0, bq), i * LANES:(i + 1) * LANES].astype(jnp.float32)
            o_ref[pl.ds(r0, bq), i * LANES:(i + 1) * LANES] = (
                o[i * bq:(i + 1) * bq] * gate).astype(o_ref.dtype)

    _attention_groups(make_qs, finish, n_groups, k_ref, v_ref, vaug_scr, acc_scr,
                      ((s0_scr, m0_scr), (s1_scr, m1_scr)))


def _attn_b(proj, batch, seq):
    q_rows = min(seq, B_SPLIT_ROWS)
    n_split = seq // q_rows
    group_cols = B_GROUP * LANES
    k_tile0 = B_HEADS
    v_tile0 = B_HEADS + B_KV_HEADS
    g_blk0 = (B_HEADS + 2 * B_KV_HEADS) * LANES // group_cols
    q_spec = lambda blk0: pl.BlockSpec(
        (q_rows, group_cols), lambda b, kv, i: (b * n_split + i, blk0 + kv))
    kv_spec = lambda tile0: pl.BlockSpec((seq, LANES), lambda b, kv, i: (b, tile0 + kv))
    return pl.pallas_call(
        _attn_b_kernel,
        grid=(batch, B_KV_HEADS, n_split),
        in_specs=[q_spec(0), kv_spec(k_tile0), kv_spec(v_tile0), q_spec(g_blk0)],
        out_specs=q_spec(0),
        out_shape=jax.ShapeDtypeStruct((batch * seq, B_HEADS * LANES), jnp.bfloat16),
        scratch_shapes=_attn_scratch(seq),
        compiler_params=_compiler_params(("parallel", "parallel", "parallel")),
        name="attn_b",
    )(proj, proj, proj, proj)


def _outproj_rows(o_ref, w_ref, nw_ref, x_ref):
    m = jnp.dot(o_ref[...], w_ref[...], preferred_element_type=jnp.float32)
    return x_ref[...] + m * _rms_scale(m) * nw_ref[...]


def _outproj_kernel(o_ref, w_ref, nw_ref, x_ref, y_ref):
    y_ref[...] = _outproj_rows(o_ref, w_ref, nw_ref, x_ref)


def _out_in_kernel(o_ref, wo_ref, nwo_ref, x_ref, nwi_ref, wi_ref, c_ref, sa_ref, sb_ref,
                   qn_ref, kn_ref, y_ref, p_ref, h_scr, *, mixer):
    @pl.when(pl.program_id(0) == 0)
    def _():
        h_scr[...] = jnp.zeros_like(h_scr)

    h_prev = h_scr[...]
    y = _outproj_rows(o_ref, wo_ref, nwo_ref, x_ref)
    y_ref[...] = y
    _inproj_rows(h_prev, wi_ref, c_ref, sa_ref, sb_ref, qn_ref, kn_ref, p_ref, mixer)
    h_scr[...] = _pre_norm(y, nwi_ref)


def _out_in(o, w_out, norm_post, x2, seq, norm_pre, w_in, tables, qn, kn, mixer):
    n_tok = x2.shape[0]
    n_out = w_in.shape[1]
    rows = PROJ_ROWS[mixer]
    n_tiles = n_tok // rows
    per_seq = seq // rows
    cur = lambda i: (jnp.minimum(i, n_tiles - 1), 0)
    prev = lambda i: (jnp.maximum(i - 1, 0), 0)
    const = lambda shape: pl.BlockSpec(shape, lambda i: (0, 0))
    tab_spec = pl.BlockSpec((rows, LANES), lambda i: (jnp.maximum(i - 1, 0) % per_seq, 0))
    return pl.pallas_call(
        functools.partial(_out_in_kernel, mixer=mixer),
        grid=(n_tiles + 1,),
        in_specs=[
            pl.BlockSpec((rows, D_MODEL), cur), const((D_MODEL, D_MODEL)),
            const((1, D_MODEL)), pl.BlockSpec((rows, D_MODEL), cur),
            const((1, D_MODEL)), const((D_MODEL, n_out)),
            tab_spec, tab_spec, tab_spec, const((1, LANES)), const((1, LANES)),
        ],
        out_specs=[pl.BlockSpec((rows, D_MODEL), cur), pl.BlockSpec((rows, n_out), prev)],
        out_shape=[jax.ShapeDtypeStruct((n_tok, D_MODEL), jnp.float32),
                   jax.ShapeDtypeStruct((n_tok, n_out), jnp.bfloat16)],
        scratch_shapes=[pltpu.VMEM((rows, D_MODEL), jnp.bfloat16)],
        compiler_params=_compiler_params(("arbitrary",)),
        name=f"out_in_{mixer}",
    )(o, w_out, norm_post.reshape(1, D_MODEL), x2, norm_pre.reshape(1, D_MODEL), w_in,
      *tables, qn.reshape(1, LANES), kn.reshape(1, LANES))


def _outproj(o, w_out, norm_w, x2):
    n_tok = x2.shape[0]
    row_spec = pl.BlockSpec((OUT_ROWS, D_MODEL), lambda i: (i, 0))
    return pl.pallas_call(
        _outproj_kernel,
        grid=(n_tok // OUT_ROWS,),
        in_specs=[
            row_spec,
            pl.BlockSpec((D_MODEL, D_MODEL), lambda i: (0, 0)),
            pl.BlockSpec((1, D_MODEL), lambda i: (0, 0)),
            row_spec,
        ],
        out_specs=row_spec,
        out_shape=jax.ShapeDtypeStruct((n_tok, D_MODEL), jnp.float32),
        compiler_params=_compiler_params(("parallel",)),
        name="outproj",
    )(o, w_out, norm_w.reshape(1, D_MODEL), x2)


def _trunk(x, norm_pre, norm_post, a_w_in, a_w_out, a_lam, a_subln,
           b_w_in, b_w_out, b_q_norm, b_k_norm):
    batch, seq, _ = x.shape
    x2 = x.reshape(batch * seq, D_MODEL)
    tables_a = _rope_tables_a(seq)
    tables_b = _rope_tables_b(seq)
    ones = jnp.ones((LANES,), jnp.float32)

    def inproj_args(i):
        j = i // 2
        if i % 2 == 0:
            return (seq, norm_pre[i], a_w_in[j], tables_a, ones, ones, "a")
        return (seq, norm_pre[i], b_w_in[j], tables_b, b_q_norm[j], b_k_norm[j], "b")

    proj = _inproj(x2, *inproj_args(0))
    for i in range(DEPTH):
        j = i // 2
        if i % 2 == 0:
            lam_init = 0.8 - 0.6 * math.exp(-0.3 * i)
            o = _attn_a(proj, batch, seq, a_lam[j], a_subln[j], lam_init)
            w_out = a_w_out[j]
        else:
            o = _attn_b(proj, batch, seq)
            w_out = b_w_out[j]
        if i + 1 < DEPTH:
            x2, proj = _out_in(o, w_out, norm_post[i], x2, *inproj_args(i + 1))
        else:
            x2 = _outproj(o, w_out, norm_post[i], x2)
    return x2.reshape(batch, seq, D_MODEL)


def kernel(x_prompt, x_sample, norm_pre, norm_post, a_w_in, a_w_out, a_lam, a_subln,
           b_w_in, b_w_out, b_q_norm, b_k_norm):
    bf16 = jnp.bfloat16
    weights = (norm_pre, norm_post, a_w_in.astype(bf16), a_w_out.astype(bf16), a_lam,
               a_subln, b_w_in.astype(bf16), b_w_out.astype(bf16), b_q_norm, b_k_norm)
    return (_trunk(x_prompt, *weights), _trunk(x_sample, *weights))
```

```python
import functools
import math

import jax
import jax.numpy as jnp
from jax import lax
from jax.experimental import pallas as pl
from jax.experimental.pallas import tpu as pltpu

D_MODEL = 1024
DEPTH = 4
NORM_EPS = 1e-6

A_HEADS = 8
A_QK_DIM = 64
A_ROT_DIM = 16
ROPE_THETA = 500000.0

B_HEADS = 8
B_KV_HEADS = 2
B_GROUP = B_HEADS // B_KV_HEADS
B_HEAD_DIM = 128
B_AXIS_DIM = 64
AXIAL_THETA = 10000.0
GRID_W = 64

LANES = 128
LOG2E = 1.4426950408889634
VMEM_LIMIT_BYTES = 56 * 1024 * 1024

PROJ_ROWS = {"a": 512, "b": 512}
PROJ_SUB_ROWS = {"a": 512, "b": 256}
FIRST_PROJ_ROWS = 1024
OUT_ROWS = 1024
PROJ_COLS = 512
ATTN_ROWS = 512
KEY_CHUNK = 512
GROUP_UNROLL = 6
B_SPLIT_ROWS = 2048


def _compiler_params(semantics):
    return pltpu.CompilerParams(
        dimension_semantics=semantics, vmem_limit_bytes=VMEM_LIMIT_BYTES)


def _angles(pos, dim, theta):
    inv_freq = theta ** (-(jnp.arange(0, dim, 2, dtype=jnp.float32) / dim))
    return pos[:, None] * inv_freq[None, :]


def _rope_tables_a(seq):
    ang = _angles(jnp.arange(seq, dtype=jnp.float32), A_ROT_DIM, ROPE_THETA)
    cos, sin = jnp.cos(ang), jnp.sin(ang)
    half = A_ROT_DIM // 2
    one = jnp.ones((seq, A_QK_DIM - A_ROT_DIM), jnp.float32)
    zero_h = jnp.zeros((seq, half), jnp.float32)
    zero_r = jnp.zeros((seq, A_QK_DIM - A_ROT_DIM), jnp.float32)
    c = jnp.concatenate([cos, cos, one], axis=1)
    sa = jnp.concatenate([-sin, zero_h, zero_r], axis=1)
    sb = jnp.concatenate([zero_h, sin, zero_r], axis=1)
    tile2 = lambda t: jnp.concatenate([t, t], axis=1)
    return tile2(c), tile2(sa), tile2(sb)


def _rope_tables_b(seq):
    rows = seq // GRID_W
    row_ids = jnp.repeat(jnp.arange(rows, dtype=jnp.float32), GRID_W)
    col_ids = jnp.tile(jnp.arange(GRID_W, dtype=jnp.float32), rows)
    ar = _angles(row_ids, B_AXIS_DIM, AXIAL_THETA)
    ac = _angles(col_ids, B_AXIS_DIM, AXIAL_THETA)
    z = jnp.zeros_like(ar)
    c = jnp.concatenate([jnp.cos(ar), jnp.cos(ar), jnp.cos(ac), jnp.cos(ac)], axis=1)
    sa = jnp.concatenate([-jnp.sin(ar), z, -jnp.sin(ac), z], axis=1)
    sb = jnp.concatenate([z, jnp.sin(ar), z, jnp.sin(ac)], axis=1)
    return c, sa, sb


def _rotate(y, c, sa, sb, shift):
    return (y * c + pltpu.roll(y, LANES - shift, axis=1) * sa
            + pltpu.roll(y, shift, axis=1) * sb)


def _silu(g):
    return g / (1.0 + jnp.exp(-g))


def _rms_scale(y):
    return lax.rsqrt(jnp.mean(y * y, axis=-1, keepdims=True) + NORM_EPS)


def _pre_norm(x, nw_ref):
    return (x * _rms_scale(x) * nw_ref[...]).astype(jnp.bfloat16)


def _inproj_kernel(x_ref, nw_ref, w_ref, c_ref, sa_ref, sb_ref, qn_ref, kn_ref,
                   o_ref, *, mixer):
    _inproj_rows(_pre_norm(x_ref[...], nw_ref), w_ref, c_ref, sa_ref, sb_ref,
                 qn_ref, kn_ref, o_ref, mixer)


def _inproj_rows(h, w_ref, c_ref, sa_ref, sb_ref, qn_ref, kn_ref, o_ref, mixer):
    sub = PROJ_SUB_ROWS[mixer]
    for r0 in range(0, h.shape[0], sub):
        rws = slice(r0, r0 + sub)
        _inproj_sub_rows(h[rws], w_ref, c_ref[rws, :], sa_ref[rws, :], sb_ref[rws, :],
                         qn_ref, kn_ref, o_ref, rws, mixer)


def _inproj_sub_rows(h, w_ref, c, sa, sb, qn_ref, kn_ref, o_ref, rws, mixer):
    n_out = o_ref.shape[1]
    if mixer == "a":
        n_q, n_k, shift = A_HEADS, A_HEADS, A_ROT_DIM // 2
        q_scale = (A_QK_DIM ** -0.5) * LOG2E
        n_plain = A_HEADS
    else:
        n_q, n_k, shift = B_HEADS, B_KV_HEADS, B_AXIS_DIM // 2
        q_scale = (B_HEAD_DIM ** -0.5) * LOG2E
        n_plain = B_KV_HEADS
    for col0 in range(0, n_out, PROJ_COLS):
        y = jnp.dot(h, w_ref[:, col0:col0 + PROJ_COLS],
                    preferred_element_type=jnp.float32)
        for t in range(PROJ_COLS // LANES):
            tile = col0 // LANES + t
            yt = y[:, t * LANES:(t + 1) * LANES]
            if tile < n_q + n_k:
                is_q = tile < n_q
                if mixer == "b":
                    yt = yt * _rms_scale(yt) * (qn_ref[...] if is_q else kn_ref[...])
                yt = _rotate(yt, c, sa, sb, shift)
                if is_q:
                    yt = yt * q_scale
            elif tile >= n_q + n_k + n_plain:
                yt = _silu(yt)
            o_ref[rws, tile * LANES:(tile + 1) * LANES] = yt.astype(o_ref.dtype)


def _inproj(x2, seq, norm_w, w_in, tables, qn, kn, mixer):
    n_tok = x2.shape[0]
    n_out = w_in.shape[1]
    rows = FIRST_PROJ_ROWS
    per_seq = seq // rows
    tab_spec = pl.BlockSpec((rows, LANES), lambda i: (i % per_seq, 0))
    vec_spec = pl.BlockSpec((1, LANES), lambda i: (0, 0))
    return pl.pallas_call(
        functools.partial(_inproj_kernel, mixer=mixer),
        grid=(n_tok // rows,),
        in_specs=[
            pl.BlockSpec((rows, D_MODEL), lambda i: (i, 0)),
            pl.BlockSpec((1, D_MODEL), lambda i: (0, 0)),
            pl.BlockSpec((D_MODEL, n_out), lambda i: (0, 0)),
            tab_spec, tab_spec, tab_spec, vec_spec, vec_spec,
        ],
        out_specs=pl.BlockSpec((rows, n_out), lambda i: (i, 0)),
        out_shape=jax.ShapeDtypeStruct((n_tok, n_out), jnp.bfloat16),
        compiler_params=_compiler_params(("parallel",)),
        name=f"inproj_{mixer}",
    )(x2, norm_w.reshape(1, D_MODEL), w_in, *tables,
      qn.reshape(1, LANES), kn.reshape(1, LANES))


def _lane_fold(x, op):
    out = x[:, 0:LANES]
    for t in range(1, x.shape[1] // LANES):
        out = op(out, x[:, t * LANES:(t + 1) * LANES])
    return out


def _attention_groups(make_qs, finish, n_groups, k_ref, v_ref, vaug_scr, acc_scr, bufs):
    n_chunks, rows, _ = bufs[0][0].shape
    vaug_scr[:, 0:LANES] = v_ref[...]
    vaug_scr[:, LANES:2 * LANES] = jnp.ones((v_ref.shape[0], LANES), vaug_scr.dtype)

    def score_chunk(qs, c, s_scr, m_run):
        s = lax.dot_general(qs, k_ref[c * KEY_CHUNK:(c + 1) * KEY_CHUNK, :],
                            (((1,), (1,)), ((), ())),
                            preferred_element_type=jnp.float32)
        s_scr[c] = s
        m_c = _lane_fold(s, jnp.maximum)
        return m_c if m_run is None else jnp.maximum(m_run, m_c)

    def weigh_chunk(c, s_scr, m_cur, acc):
        s = s_scr[c]
        p = jnp.concatenate(
            [jnp.exp2(s[:, t * LANES:(t + 1) * LANES] - m_cur)
             for t in range(KEY_CHUNK // LANES)], axis=1).astype(jnp.bfloat16)
        pv = jnp.dot(p, vaug_scr[c * KEY_CHUNK:(c + 1) * KEY_CHUNK, :],
                     preferred_element_type=jnp.float32)
        return pv if acc is None else acc + pv

    def stage(g_cur, g_next, parity, defer_finish=False):
        s_cur, m_ref = bufs[parity]
        s_next, m_next = bufs[1 - parity]
        if g_next is not None:
            qs_next = make_qs(g_next)
        if g_cur is not None:
            m_cur = m_ref[...]
        m_run, acc = None, None
        for c in range(n_chunks):
            if g_next is not None:
                m_run = score_chunk(qs_next, c, s_next, m_run)
            if g_cur is not None:
                acc = weigh_chunk(c, s_cur, m_cur, acc)
        if g_next is not None:
            m = jnp.max(m_run, axis=-1, keepdims=True)
            m_next[...] = jnp.broadcast_to(m, (rows, LANES))
        if g_cur is not None:
            if defer_finish:
                acc_scr[...] = acc
            else:
                finish(g_cur, acc)

    stage(None, 0, 1)
    stage(0, 1, 0, defer_finish=True)

    def stages(g, count):
        finish(g, acc_scr[...])
        for u in range(1, count + 1):
            stage(g + u, g + u + 1, u % 2, defer_finish=(u == count))

    def body(j, carry):
        stages(GROUP_UNROLL * j, GROUP_UNROLL)
        return carry

    n_loop = (n_groups - 2) // GROUP_UNROLL
    lax.fori_loop(0, n_loop, body, 0)
    done = GROUP_UNROLL * n_loop
    if n_groups - 2 > done:
        stages(done, n_groups - 2 - done)
    finish(n_groups - 2, acc_scr[...])
    stage(n_groups - 1, None, 1)


def _row_start(g, bq):
    return g * bq if isinstance(g, int) else pl.multiple_of(g * bq, bq)


def _attn_a_kernel(lam_ref, subln_ref, q_ref, k_ref, v_ref, g_ref, o_ref,
                   vaug_scr, acc_scr, s0_scr, m0_scr, s1_scr, m1_scr, *, lam_init):
    bq = ATTN_ROWS // 2
    n_groups = q_ref.shape[0] // bq
    lam = lam_ref[...]
    lam_full = (jnp.exp(jnp.sum(lam[0:1] * lam[1:2], axis=-1, keepdims=True))
                - jnp.exp(jnp.sum(lam[2:3] * lam[3:4], axis=-1, keepdims=True))
                + lam_init)
    subln = subln_ref[...] * (1.0 - lam_init)

    def make_qs(g):
        q = q_ref[pl.ds(_row_start(g, bq), bq), :]
        lane = lax.broadcasted_iota(jnp.int32, q.shape, 1)
        zero = jnp.zeros_like(q)
        return jnp.concatenate([jnp.where(lane < A_QK_DIM, q, zero),
                                jnp.where(lane >= A_QK_DIM, q, zero)], axis=0)

    def finish(g, acc):
        r0 = _row_start(g, bq)
        o = acc[:, 0:LANES] / acc[:, LANES:2 * LANES]
        o = o[:bq] - lam_full * o[bq:]
        o = o * _rms_scale(o) * subln
        gate = g_ref[pl.ds(r0, bq), :].astype(jnp.float32)
        o_ref[pl.ds(r0, bq), :] = (o * gate).astype(o_ref.dtype)

    _attention_groups(make_qs, finish, n_groups, k_ref, v_ref, vaug_scr, acc_scr,
                      ((s0_scr, m0_scr), (s1_scr, m1_scr)))


def _attn_scratch(seq):
    n_chunks = seq // KEY_CHUNK
    scores = pltpu.VMEM((n_chunks, ATTN_ROWS, KEY_CHUNK), jnp.float32)
    maxima = pltpu.VMEM((ATTN_ROWS, LANES), jnp.float32)
    return [pltpu.VMEM((seq, 2 * LANES), jnp.bfloat16),
            pltpu.VMEM((ATTN_ROWS, 2 * LANES), jnp.float32),
            scores, maxima, scores, maxima]


def _attn_a(proj, batch, seq, lam, subln, lam_init):
    head_spec = lambda tile0: pl.BlockSpec((seq, LANES), lambda b, h: (b, tile0 + h))
    return pl.pallas_call(
        functools.partial(_attn_a_kernel, lam_init=lam_init),
        grid=(batch, A_HEADS),
        in_specs=[
            pl.BlockSpec((4, A_QK_DIM), lambda b, h: (0, 0)),
            pl.BlockSpec((1, LANES), lambda b, h: (0, 0)),
            head_spec(0), head_spec(A_HEADS), head_spec(2 * A_HEADS), head_spec(3 * A_HEADS),
        ],
        out_specs=head_spec(0),
        out_shape=jax.ShapeDtypeStruct((batch * seq, A_HEADS * LANES), jnp.bfloat16),
        scratch_shapes=_attn_scratch(seq),
        compiler_params=_compiler_params(("parallel", "parallel")),
        name="attn_a",
    )(lam, subln.reshape(1, LANES), proj, proj, proj, proj)


def _attn_b_kernel(q_ref, k_ref, v_ref, g_ref, o_ref,
                   vaug_scr, acc_scr, s0_scr, m0_scr, s1_scr, m1_scr):
    bq = ATTN_ROWS // B_GROUP
    n_groups = q_ref.shape[0] // bq

    def make_qs(g):
        r0 = _row_start(g, bq)
        return jnp.concatenate(
            [q_ref[pl.ds(r0, bq), i * LANES:(i + 1) * LANES] for i in range(B_GROUP)], axis=0)

    def finish(g, acc):
        r0 = _row_start(g, bq)
        o = acc[:, 0:LANES] / acc[:, LANES:2 * LANES]
        for i in range(B_GROUP):
            gate = g_ref[pl.ds(r0, bq), i * LANES:(i + 1) * LANES].astype(jnp.float32)
            o_ref[pl.ds(r0, bq), i * LANES:(i + 1) * LANES] = (
                o[i * bq:(i + 1) * bq] * gate).astype(o_ref.dtype)

    _attention_groups(make_qs, finish, n_groups, k_ref, v_ref, vaug_scr, acc_scr,
                      ((s0_scr, m0_scr), (s1_scr, m1_scr)))


def _attn_b(proj, batch, seq):
    q_rows = min(seq, B_SPLIT_ROWS)
    n_split = seq // q_rows
    group_cols = B_GROUP * LANES
    k_tile0 = B_HEADS
    v_tile0 = B_HEADS + B_KV_HEADS
    g_blk0 = (B_HEADS + 2 * B_KV_HEADS) * LANES // group_cols
    q_spec = lambda blk0: pl.BlockSpec(
        (q_rows, group_cols), lambda b, kv, i: (b * n_split + i, blk0 + kv))
    kv_spec = lambda tile0: pl.BlockSpec((seq, LANES), lambda b, kv, i: (b, tile0 + kv))
    return pl.pallas_call(
        _attn_b_kernel,
        grid=(batch, B_KV_HEADS, n_split),
        in_specs=[q_spec(0), kv_spec(k_tile0), kv_spec(v_tile0), q_spec(g_blk0)],
        out_specs=q_spec(0),
        out_shape=jax.ShapeDtypeStruct((batch * seq, B_HEADS * LANES), jnp.bfloat16),
        scratch_shapes=_attn_scratch(seq),
        compiler_params=_compiler_params(("parallel", "parallel", "parallel")),
        name="attn_b",
    )(proj, proj, proj, proj)


def _outproj_rows(o_ref, w_ref, nw_ref, x_ref):
    m = jnp.dot(o_ref[...], w_ref[...], preferred_element_type=jnp.float32)
    return x_ref[...] + m * _rms_scale(m) * nw_ref[...]


def _outproj_kernel(o_ref, w_ref, nw_ref, x_ref, y_ref):
    y_ref[...] = _outproj_rows(o_ref, w_ref, nw_ref, x_ref)


def _out_in_kernel(o_ref, wo_ref, nwo_ref, x_ref, nwi_ref, wi_ref, c_ref, sa_ref, sb_ref,
                   qn_ref, kn_ref, y_ref, p_ref, h_scr, *, mixer):
    @pl.when(pl.program_id(0) == 0)
    def _():
        h_scr[...] = jnp.zeros_like(h_scr)

    h_prev = h_scr[...]
    y = _outproj_rows(o_ref, wo_ref, nwo_ref, x_ref)
    y_ref[...] = y
    _inproj_rows(h_prev, wi_ref, c_ref, sa_ref, sb_ref, qn_ref, kn_ref, p_ref, mixer)
    h_scr[...] = _pre_norm(y, nwi_ref)


def _out_in(o, w_out, norm_post, x2, seq, norm_pre, w_in, tables, qn, kn, mixer):
    n_tok = x2.shape[0]
    n_out = w_in.shape[1]
    rows = PROJ_ROWS[mixer]
    n_tiles = n_tok // rows
    per_seq = seq // rows
    cur = lambda i: (jnp.minimum(i, n_tiles - 1), 0)
    prev = lambda i: (jnp.maximum(i - 1, 0), 0)
    const = lambda shape: pl.BlockSpec(shape, lambda i: (0, 0))
    tab_spec = pl.BlockSpec((rows, LANES), lambda i: (jnp.maximum(i - 1, 0) % per_seq, 0))
    return pl.pallas_call(
        functools.partial(_out_in_kernel, mixer=mixer),
        grid=(n_tiles + 1,),
        in_specs=[
            pl.BlockSpec((rows, D_MODEL), cur), const((D_MODEL, D_MODEL)),
            const((1, D_MODEL)), pl.BlockSpec((rows, D_MODEL), cur),
            const((1, D_MODEL)), const((D_MODEL, n_out)),
            tab_spec, tab_spec, tab_spec, const((1, LANES)), const((1, LANES)),
        ],
        out_specs=[pl.BlockSpec((rows, D_MODEL), cur), pl.BlockSpec((rows, n_out), prev)],
        out_shape=[jax.ShapeDtypeStruct((n_tok, D_MODEL), jnp.float32),
                   jax.ShapeDtypeStruct((n_tok, n_out), jnp.bfloat16)],
        scratch_shapes=[pltpu.VMEM((rows, D_MODEL), jnp.bfloat16)],
        compiler_params=_compiler_params(("arbitrary",)),
        name=f"out_in_{mixer}",
    )(o, w_out, norm_post.reshape(1, D_MODEL), x2, norm_pre.reshape(1, D_MODEL), w_in,
      *tables, qn.reshape(1, LANES), kn.reshape(1, LANES))


def _outproj(o, w_out, norm_w, x2):
    n_tok = x2.shape[0]
    row_spec = pl.BlockSpec((OUT_ROWS, D_MODEL), lambda i: (i, 0))
    return pl.pallas_call(
        _outproj_kernel,
        grid=(n_tok // OUT_ROWS,),
        in_specs=[
            row_spec,
            pl.BlockSpec((D_MODEL, D_MODEL), lambda i: (0, 0)),
            pl.BlockSpec((1, D_MODEL), lambda i: (0, 0)),
            row_spec,
        ],
        out_specs=row_spec,
        out_shape=jax.ShapeDtypeStruct((n_tok, D_MODEL), jnp.float32),
        compiler_params=_compiler_params(("parallel",)),
        name="outproj",
    )(o, w_out, norm_w.reshape(1, D_MODEL), x2)


def _trunk(x, norm_pre, norm_post, a_w_in, a_w_out, a_lam, a_subln,
           b_w_in, b_w_out, b_q_norm, b_k_norm):
    batch, seq, _ = x.shape
    x2 = x.reshape(batch * seq, D_MODEL)
    tables_a = _rope_tables_a(seq)
    tables_b = _rope_tables_b(seq)
    ones = jnp.ones((LANES,), jnp.float32)

    def inproj_args(i):
        j = i // 2
        if i % 2 == 0:
            return (seq, norm_pre[i], a_w_in[j], tables_a, ones, ones, "a")
        return (seq, norm_pre[i], b_w_in[j], tables_b, b_q_norm[j], b_k_norm[j], "b")

    proj = _inproj(x2, *inproj_args(0))
    for i in range(DEPTH):
        j = i // 2
        if i % 2 == 0:
            lam_init = 0.8 - 0.6 * math.exp(-0.3 * i)
            o = _attn_a(proj, batch, seq, a_lam[j], a_subln[j], lam_init)
            w_out = a_w_out[j]
        else:
            o = _attn_b(proj, batch, seq)
            w_out = b_w_out[j]
        if i + 1 < DEPTH:
            x2, proj = _out_in(o, w_out, norm_post[i], x2, *inproj_args(i + 1))
        else:
            x2 = _outproj(o, w_out, norm_post[i], x2)
    return x2.reshape(batch, seq, D_MODEL)


def kernel(x_prompt, x_sample, norm_pre, norm_post, a_w_in, a_w_out, a_lam, a_subln,
           b_w_in, b_w_out, b_q_norm, b_k_norm):
    bf16 = jnp.bfloat16
    weights = (norm_pre, norm_post, a_w_in.astype(bf16), a_w_out.astype(bf16), a_lam,
               a_subln, b_w_in.astype(bf16), b_w_out.astype(bf16), b_q_norm, b_k_norm)
    return (_trunk(x_prompt, *weights), _trunk(x_sample, *weights))
```

```python
import functools
import math

import jax
import jax.numpy as jnp
from jax import lax
from jax.experimental import pallas as pl
from jax.experimental.pallas import tpu as pltpu

D_MODEL = 1024
DEPTH = 4
NORM_EPS = 1e-6

A_HEADS = 8
A_QK_DIM = 64
A_ROT_DIM = 16
ROPE_THETA = 500000.0

B_HEADS = 8
B_KV_HEADS = 2
B_GROUP = B_HEADS // B_KV_HEADS
B_HEAD_DIM = 128
B_AXIS_DIM = 64
AXIAL_THETA = 10000.0
GRID_W = 64

LANES = 128
LOG2E = 1.4426950408889634
VMEM_LIMIT_BYTES = 56 * 1024 * 1024

PROJ_ROWS = {"a": 512, "b": 512}
PROJ_SUB_ROWS = {"a": 512, "b": 256}
FIRST_PROJ_ROWS = 1024
OUT_ROWS = 1024
PROJ_COLS = 512
ATTN_ROWS = 512
KEY_CHUNK = 256
GROUP_UNROLL = 6
B_SPLIT_ROWS = 2048


def _compiler_params(semantics):
    return pltpu.CompilerParams(
        dimension_semantics=semantics, vmem_limit_bytes=VMEM_LIMIT_BYTES)


def _angles(pos, dim, theta):
    inv_freq = theta ** (-(jnp.arange(0, dim, 2, dtype=jnp.float32) / dim))
    return pos[:, None] * inv_freq[None, :]


def _rope_tables_a(seq):
    ang = _angles(jnp.arange(seq, dtype=jnp.float32), A_ROT_DIM, ROPE_THETA)
    cos, sin = jnp.cos(ang), jnp.sin(ang)
    half = A_ROT_DIM // 2
    one = jnp.ones((seq, A_QK_DIM - A_ROT_DIM), jnp.float32)
    zero_h = jnp.zeros((seq, half), jnp.float32)
    zero_r = jnp.zeros((seq, A_QK_DIM - A_ROT_DIM), jnp.float32)
    c = jnp.concatenate([cos, cos, one], axis=1)
    sa = jnp.concatenate([-sin, zero_h, zero_r], axis=1)
    sb = jnp.concatenate([zero_h, sin, zero_r], axis=1)
    tile2 = lambda t: jnp.concatenate([t, t], axis=1)
    return tile2(c), tile2(sa), tile2(sb)


def _rope_tables_b(seq):
    rows = seq // GRID_W
    row_ids = jnp.repeat(jnp.arange(rows, dtype=jnp.float32), GRID_W)
    col_ids = jnp.tile(jnp.arange(GRID_W, dtype=jnp.float32), rows)
    ar = _angles(row_ids, B_AXIS_DIM, AXIAL_THETA)
    ac = _angles(col_ids, B_AXIS_DIM, AXIAL_THETA)
    z = jnp.zeros_like(ar)
    c = jnp.concatenate([jnp.cos(ar), jnp.cos(ar), jnp.cos(ac), jnp.cos(ac)], axis=1)
    sa = jnp.concatenate([-jnp.sin(ar), z, -jnp.sin(ac), z], axis=1)
    sb = jnp.concatenate([z, jnp.sin(ar), z, jnp.sin(ac)], axis=1)
    return c, sa, sb


def _rotate(y, c, sa, sb, shift):
    return (y * c + pltpu.roll(y, LANES - shift, axis=1) * sa
            + pltpu.roll(y, shift, axis=1) * sb)


def _silu(g):
    return g / (1.0 + jnp.exp(-g))


def _rms_scale(y):
    return lax.rsqrt(jnp.mean(y * y, axis=-1, keepdims=True) + NORM_EPS)


def _pre_norm(x, nw_ref):
    return (x * _rms_scale(x) * nw_ref[...]).astype(jnp.bfloat16)


def _inproj_kernel(x_ref, nw_ref, w_ref, c_ref, sa_ref, sb_ref, qn_ref, kn_ref,
                   o_ref, *, mixer):
    _inproj_rows(_pre_norm(x_ref[...], nw_ref), w_ref, c_ref, sa_ref, sb_ref,
                 qn_ref, kn_ref, o_ref, mixer)


def _inproj_rows(h, w_ref, c_ref, sa_ref, sb_ref, qn_ref, kn_ref, o_ref, mixer):
    sub = PROJ_SUB_ROWS[mixer]
    for r0 in range(0, h.shape[0], sub):
        rws = slice(r0, r0 + sub)
        _inproj_sub_rows(h[rws], w_ref, c_ref[rws, :], sa_ref[rws, :], sb_ref[rws, :],
                         qn_ref, kn_ref, o_ref, rws, mixer)


def _inproj_sub_rows(h, w_ref, c, sa, sb, qn_ref, kn_ref, o_ref, rws, mixer):
    n_out = o_ref.shape[1]
    if mixer == "a":
        n_q, n_k, shift = A_HEADS, A_HEADS, A_ROT_DIM // 2
        q_scale = (A_QK_DIM ** -0.5) * LOG2E
        n_plain = A_HEADS
    else:
        n_q, n_k, shift = B_HEADS, B_KV_HEADS, B_AXIS_DIM // 2
        q_scale = (B_HEAD_DIM ** -0.5) * LOG2E
        n_plain = B_KV_HEADS
    for col0 in range(0, n_out, PROJ_COLS):
        y = jnp.dot(h, w_ref[:, col0:col0 + PROJ_COLS],
                    preferred_element_type=jnp.float32)
        for t in range(PROJ_COLS // LANES):
            tile = col0 // LANES + t
            yt = y[:, t * LANES:(t + 1) * LANES]
            if tile < n_q + n_k:
                is_q = tile < n_q
                if mixer == "b":
                    yt = yt * _rms_scale(yt) * (qn_ref[...] if is_q else kn_ref[...])
                yt = _rotate(yt, c, sa, sb, shift)
                if is_q:
                    yt = yt * q_scale
            elif tile >= n_q + n_k + n_plain:
                yt = _silu(yt)
            o_ref[rws, tile * LANES:(tile + 1) * LANES] = yt.astype(o_ref.dtype)


def _inproj(x2, seq, norm_w, w_in, tables, qn, kn, mixer):
    n_tok = x2.shape[0]
    n_out = w_in.shape[1]
    rows = FIRST_PROJ_ROWS
    per_seq = seq // rows
    tab_spec = pl.BlockSpec((rows, LANES), lambda i: (i % per_seq, 0))
    vec_spec = pl.BlockSpec((1, LANES), lambda i: (0, 0))
    return pl.pallas_call(
        functools.partial(_inproj_kernel, mixer=mixer),
        grid=(n_tok // rows,),
        in_specs=[
            pl.BlockSpec((rows, D_MODEL), lambda i: (i, 0)),
            pl.BlockSpec((1, D_MODEL), lambda i: (0, 0)),
            pl.BlockSpec((D_MODEL, n_out), lambda i: (0, 0)),
            tab_spec, tab_spec, tab_spec, vec_spec, vec_spec,
        ],
        out_specs=pl.BlockSpec((rows, n_out), lambda i: (i, 0)),
        out_shape=jax.ShapeDtypeStruct((n_tok, n_out), jnp.bfloat16),
        compiler_params=_compiler_params(("parallel",)),
        name=f"inproj_{mixer}",
    )(x2, norm_w.reshape(1, D_MODEL), w_in, *tables,
      qn.reshape(1, LANES), kn.reshape(1, LANES))


def _lane_fold(x, op):
    out = x[:, 0:LANES]
    for t in range(1, x.shape[1] // LANES):
        out = op(out, x[:, t * LANES:(t + 1) * LANES])
    return out


def _attention_groups(make_qs, finish, n_groups, k_ref, v_ref, vaug_scr, acc_scr, bufs):
    n_chunks, rows, _ = bufs[0][0].shape
    vaug_scr[:, 0:LANES] = v_ref[...]
    vaug_scr[:, LANES:2 * LANES] = jnp.ones((v_ref.shape[0], LANES), vaug_scr.dtype)

    def score_chunk(qs, c, s_scr, m_run):
        s = lax.dot_general(qs, k_ref[c * KEY_CHUNK:(c + 1) * KEY_CHUNK, :],
                            (((1,), (1,)), ((), ())),
                            preferred_element_type=jnp.float32)
        s_scr[c] = s
        m_c = _lane_fold(s, jnp.maximum)
        return m_c if m_run is None else jnp.maximum(m_run, m_c)

    def weigh_chunk(c, s_scr, m_cur, acc):
        s = s_scr[c]
        p = jnp.concatenate(
            [jnp.exp2(s[:, t * LANES:(t + 1) * LANES] - m_cur)
             for t in range(KEY_CHUNK // LANES)], axis=1).astype(jnp.bfloat16)
        pv = jnp.dot(p, vaug_scr[c * KEY_CHUNK:(c + 1) * KEY_CHUNK, :],
                     preferred_element_type=jnp.float32)
        return pv if acc is None else acc + pv

    def stage(g_cur, g_next, parity, defer_finish=False):
        s_cur, m_ref = bufs[parity]
        s_next, m_next = bufs[1 - parity]
        if g_next is not None:
            qs_next = make_qs(g_next)
        if g_cur is not None:
            m_cur = m_ref[...]
        m_run, acc = None, None
        for c in range(n_chunks):
            if g_next is not None:
                m_run = score_chunk(qs_next, c, s_next, m_run)
            if g_cur is not None:
                acc = weigh_chunk(c, s_cur, m_cur, acc)
        if g_next is not None:
            m = jnp.max(m_run, axis=-1, keepdims=True)
            m_next[...] = jnp.broadcast_to(m, (rows, LANES))
        if g_cur is not None:
            if defer_finish:
                acc_scr[...] = acc
            else:
                finish(g_cur, acc)

    stage(None, 0, 1)
    stage(0, 1, 0, defer_finish=True)

    def stages(g, count):
        finish(g, acc_scr[...])
        for u in range(1, count + 1):
            stage(g + u, g + u + 1, u % 2, defer_finish=(u == count))

    def body(j, carry):
        stages(GROUP_UNROLL * j, GROUP_UNROLL)
        return carry

    n_loop = (n_groups - 2) // GROUP_UNROLL
    lax.fori_loop(0, n_loop, body, 0)
    done = GROUP_UNROLL * n_loop
    if n_groups - 2 > done:
        stages(done, n_groups - 2 - done)
    finish(n_groups - 2, acc_scr[...])
    stage(n_groups - 1, None, 1)


def _row_start(g, bq):
    return g * bq if isinstance(g, int) else pl.multiple_of(g * bq, bq)


def _attn_a_kernel(lam_ref, subln_ref, q_ref, k_ref, v_ref, g_ref, o_ref,
                   vaug_scr, acc_scr, s0_scr, m0_scr, s1_scr, m1_scr, *, lam_init):
    bq = ATTN_ROWS // 2
    n_groups = q_ref.shape[0] // bq
    lam = lam_ref[...]
    lam_full = (jnp.exp(jnp.sum(lam[0:1] * lam[1:2], axis=-1, keepdims=True))
                - jnp.exp(jnp.sum(lam[2:3] * lam[3:4], axis=-1, keepdims=True))
                + lam_init)
    subln = subln_ref[...] * (1.0 - lam_init)

    def make_qs(g):
        q = q_ref[pl.ds(_row_start(g, bq), bq), :]
        lane = lax.broadcasted_iota(jnp.int32, q.shape, 1)
        zero = jnp.zeros_like(q)
        return jnp.concatenate([jnp.where(lane < A_QK_DIM, q, zero),
                                jnp.where(lane >= A_QK_DIM, q, zero)], axis=0)

    def finish(g, acc):
        r0 = _row_start(g, bq)
        o = acc[:, 0:LANES] / acc[:, LANES:2 * LANES]
        o = o[:bq] - lam_full * o[bq:]
        o = o * _rms_scale(o) * subln
        gate = g_ref[pl.ds(r0, bq), :].astype(jnp.float32)
        o_ref[pl.ds(r0, bq), :] = (o * gate).astype(o_ref.dtype)

    _attention_groups(make_qs, finish, n_groups, k_ref, v_ref, vaug_scr, acc_scr,
                      ((s0_scr, m0_scr), (s1_scr, m1_scr)))


def _attn_scratch(seq):
    n_chunks = seq // KEY_CHUNK
    scores = pltpu.VMEM((n_chunks, ATTN_ROWS, KEY_CHUNK), jnp.float32)
    maxima = pltpu.VMEM((ATTN_ROWS, LANES), jnp.float32)
    return [pltpu.VMEM((seq, 2 * LANES), jnp.bfloat16),
            pltpu.VMEM((ATTN_ROWS, 2 * LANES), jnp.float32),
            scores, maxima, scores, maxima]


def _attn_a(proj, batch, seq, lam, subln, lam_init):
    head_spec = lambda tile0: pl.BlockSpec((seq, LANES), lambda b, h: (b, tile0 + h))
    return pl.pallas_call(
        functools.partial(_attn_a_kernel, lam_init=lam_init),
        grid=(batch, A_HEADS),
        in_specs=[
            pl.BlockSpec((4, A_QK_DIM), lambda b, h: (0, 0)),
            pl.BlockSpec((1, LANES), lambda b, h: (0, 0)),
            head_spec(0), head_spec(A_HEADS), head_spec(2 * A_HEADS), head_spec(3 * A_HEADS),
        ],
        out_specs=head_spec(0),
        out_shape=jax.ShapeDtypeStruct((batch * seq, A_HEADS * LANES), jnp.bfloat16),
        scratch_shapes=_attn_scratch(seq),
        compiler_params=_compiler_params(("parallel", "parallel")),
        name="attn_a",
    )(lam, subln.reshape(1, LANES), proj, proj, proj, proj)


def _attn_b_kernel(q_ref, k_ref, v_ref, g_ref, o_ref,
                   vaug_scr, acc_scr, s0_scr, m0_scr, s1_scr, m1_scr):
    bq = ATTN_ROWS // B_GROUP
    n_groups = q_ref.shape[0] // bq

    def make_qs(g):
        r0 = _row_start(g, bq)
        return jnp.concatenate(
            [q_ref[pl.ds(r0, bq), i * LANES:(i + 1) * LANES] for i in range(B_GROUP)], axis=0)

    def finish(g, acc):
        r0 = _row_start(g, bq)
        o = acc[:, 0:LANES] / acc[:, LANES:2 * LANES]
        for i in range(B_GROUP):
            gate = g_ref[pl.ds(r0, bq), i * LANES:(i + 1) * LANES].astype(jnp.float32)
            o_ref[pl.ds(r0, bq), i * LANES:(i + 1) * LANES] = (
                o[i * bq:(i + 1) * bq] * gate).astype(o_ref.dtype)

    _attention_groups(make_qs, finish, n_groups, k_ref, v_ref, vaug_scr, acc_scr,
                      ((s0_scr, m0_scr), (s1_scr, m1_scr)))


def _attn_b(proj, batch, seq):
    q_rows = min(seq, B_SPLIT_ROWS)
    n_split = seq // q_rows
    group_cols = B_GROUP * LANES
    k_tile0 = B_HEADS
    v_tile0 = B_HEADS + B_KV_HEADS
    g_blk0 = (B_HEADS + 2 * B_KV_HEADS) * LANES // group_cols
    q_spec = lambda blk0: pl.BlockSpec(
        (q_rows, group_cols), lambda b, kv, i: (b * n_split + i, blk0 + kv))
    kv_spec = lambda tile0: pl.BlockSpec((seq, LANES), lambda b, kv, i: (b, tile0 + kv))
    return pl.pallas_call(
        _attn_b_kernel,
        grid=(batch, B_KV_HEADS, n_split),
        in_specs=[q_spec(0), kv_spec(k_tile0), kv_spec(v_tile0), q_spec(g_blk0)],
        out_specs=q_spec(0),
        out_shape=jax.ShapeDtypeStruct((batch * seq, B_HEADS * LANES), jnp.bfloat16),
        scratch_shapes=_attn_scratch(seq),
        compiler_params=_compiler_params(("parallel", "parallel", "parallel")),
        name="attn_b",
    )(proj, proj, proj, proj)


def _outproj_rows(o_ref, w_ref, nw_ref, x_ref):
    m = jnp.dot(o_ref[...], w_ref[...], preferred_element_type=jnp.float32)
    return x_ref[...] + m * _rms_scale(m) * nw_ref[...]


def _outproj_kernel(o_ref, w_ref, nw_ref, x_ref, y_ref):
    y_ref[...] = _outproj_rows(o_ref, w_ref, nw_ref, x_ref)


def _out_in_kernel(o_ref, wo_ref, nwo_ref, x_ref, nwi_ref, wi_ref, c_ref, sa_ref, sb_ref,
                   qn_ref, kn_ref, y_ref, p_ref, h_scr, *, mixer):
    @pl.when(pl.program_id(0) == 0)
    def _():
        h_scr[...] = jnp.zeros_like(h_scr)

    h_prev = h_scr[...]
    y = _outproj_rows(o_ref, wo_ref, nwo_ref, x_ref)
    y_ref[...] = y
    _inproj_rows(h_prev, wi_ref, c_ref, sa_ref, sb_ref, qn_ref, kn_ref, p_ref, mixer)
    h_scr[...] = _pre_norm(y, nwi_ref)


def _out_in(o, w_out, norm_post, x2, seq, norm_pre, w_in, tables, qn, kn, mixer):
    n_tok = x2.shape[0]
    n_out = w_in.shape[1]
    rows = PROJ_ROWS[mixer]
    n_tiles = n_tok // rows
    per_seq = seq // rows
    cur = lambda i: (jnp.minimum(i, n_tiles - 1), 0)
    prev = lambda i: (jnp.maximum(i - 1, 0), 0)
    const = lambda shape: pl.BlockSpec(shape, lambda i: (0, 0))
    tab_spec = pl.BlockSpec((rows, LANES), lambda i: (jnp.maximum(i - 1, 0) % per_seq, 0))
    return pl.pallas_call(
        functools.partial(_out_in_kernel, mixer=mixer),
        grid=(n_tiles + 1,),
        in_specs=[
            pl.BlockSpec((rows, D_MODEL), cur), const((D_MODEL, D_MODEL)),
            const((1, D_MODEL)), pl.BlockSpec((rows, D_MODEL), cur),
            const((1, D_MODEL)), const((D_MODEL, n_out)),
            tab_spec, tab_spec, tab_spec, const((1, LANES)), const((1, LANES)),
        ],
        out_specs=[pl.BlockSpec((rows, D_MODEL), cur), pl.BlockSpec((rows, n_out), prev)],
        out_shape=[jax.ShapeDtypeStruct((n_tok, D_MODEL), jnp.float32),
                   jax.ShapeDtypeStruct((n_tok, n_out), jnp.bfloat16)],
        scratch_shapes=[pltpu.VMEM((rows, D_MODEL), jnp.bfloat16)],
        compiler_params=_compiler_params(("arbitrary",)),
        name=f"out_in_{mixer}",
    )(o, w_out, norm_post.reshape(1, D_MODEL), x2, norm_pre.reshape(1, D_MODEL), w_in,
      *tables, qn.reshape(1, LANES), kn.reshape(1, LANES))


def _outproj(o, w_out, norm_w, x2):
    n_tok = x2.shape[0]
    row_spec = pl.BlockSpec((OUT_ROWS, D_MODEL), lambda i: (i, 0))
    return pl.pallas_call(
        _outproj_kernel,
        grid=(n_tok // OUT_ROWS,),
        in_specs=[
            row_spec,
            pl.BlockSpec((D_MODEL, D_MODEL), lambda i: (0, 0)),
            pl.BlockSpec((1, D_MODEL), lambda i: (0, 0)),
            row_spec,
        ],
        out_specs=row_spec,
        out_shape=jax.ShapeDtypeStruct((n_tok, D_MODEL), jnp.float32),
        compiler_params=_compiler_params(("parallel",)),
        name="outproj",
    )(o, w_out, norm_w.reshape(1, D_MODEL), x2)


def _trunk(x, norm_pre, norm_post, a_w_in, a_w_out, a_lam, a_subln,
           b_w_in, b_w_out, b_q_norm, b_k_norm):
    batch, seq, _ = x.shape
    x2 = x.reshape(batch * seq, D_MODEL)
    tables_a = _rope_tables_a(seq)
    tables_b = _rope_tables_b(seq)
    ones = jnp.ones((LANES,), jnp.float32)

    def inproj_args(i):
        j = i // 2
        if i % 2 == 0:
            return (seq, norm_pre[i], a_w_in[j], tables_a, ones, ones, "a")
        return (seq, norm_pre[i], b_w_in[j], tables_b, b_q_norm[j], b_k_norm[j], "b")

    proj = _inproj(x2, *inproj_args(0))
    for i in range(DEPTH):
        j = i // 2
        if i % 2 == 0:
            lam_init = 0.8 - 0.6 * math.exp(-0.3 * i)
            o = _attn_a(proj, batch, seq, a_lam[j], a_subln[j], lam_init)
            w_out = a_w_out[j]
        else:
            o = _attn_b(proj, batch, seq)
            w_out = b_w_out[j]
        if i + 1 < DEPTH:
            x2, proj = _out_in(o, w_out, norm_post[i], x2, *inproj_args(i + 1))
        else:
            x2 = _outproj(o, w_out, norm_post[i], x2)
    return x2.reshape(batch, seq, D_MODEL)


def kernel(x_prompt, x_sample, norm_pre, norm_post, a_w_in, a_w_out, a_lam, a_subln,
           b_w_in, b_w_out, b_q_norm, b_k_norm):
    bf16 = jnp.bfloat16
    weights = (norm_pre, norm_post, a_w_in.astype(bf16), a_w_out.astype(bf16), a_lam,
               a_subln, b_w_in.astype(bf16), b_w_out.astype(bf16), b_q_norm, b_k_norm)
    return (_trunk(x_prompt, *weights), _trunk(x_sample, *weights))
```

```python
import functools
import math

import jax
import jax.numpy as jnp
from jax import lax
from jax.experimental import pallas as pl
from jax.experimental.pallas import tpu as pltpu

D_MODEL = 1024
DEPTH = 4
NORM_EPS = 1e-6

A_HEADS = 8
A_QK_DIM = 64
A_ROT_DIM = 16
ROPE_THETA = 500000.0

B_HEADS = 8
B_KV_HEADS = 2
B_GROUP = B_HEADS // B_KV_HEADS
B_HEAD_DIM = 128
B_AXIS_DIM = 64
AXIAL_THETA = 10000.0
GRID_W = 64

LANES = 128
SUBLANES = 8
V_ROWS = LANES + 16
LOG2E = 1.4426950408889634
VMEM_LIMIT_BYTES = 56 * 1024 * 1024

PROJ_ROWS = {"a": 512, "b": 512}
PROJ_SUB_ROWS = {"a": 512, "b": 256}
FIRST_PROJ_ROWS = 1024
OUT_ROWS = 1024
PROJ_COLS = 512
ATTN_ROWS = 512
KEY_CHUNK = 512
GROUP_UNROLL = 6
B_SPLIT_ROWS = 2048


def _compiler_params(semantics):
    return pltpu.CompilerParams(
        dimension_semantics=semantics, vmem_limit_bytes=VMEM_LIMIT_BYTES)


def _angles(pos, dim, theta):
    inv_freq = theta ** (-(jnp.arange(0, dim, 2, dtype=jnp.float32) / dim))
    return pos[:, None] * inv_freq[None, :]


def _rope_tables_a(seq):
    ang = _angles(jnp.arange(seq, dtype=jnp.float32), A_ROT_DIM, ROPE_THETA)
    cos, sin = jnp.cos(ang), jnp.sin(ang)
    half = A_ROT_DIM // 2
    one = jnp.ones((seq, A_QK_DIM - A_ROT_DIM), jnp.float32)
    zero_h = jnp.zeros((seq, half), jnp.float32)
    zero_r = jnp.zeros((seq, A_QK_DIM - A_ROT_DIM), jnp.float32)
    c = jnp.concatenate([cos, cos, one], axis=1)
    sa = jnp.concatenate([-sin, zero_h, zero_r], axis=1)
    sb = jnp.concatenate([zero_h, sin, zero_r], axis=1)
    tile2 = lambda t: jnp.concatenate([t, t], axis=1)
    return tile2(c), tile2(sa), tile2(sb)


def _rope_tables_b(seq):
    rows = seq // GRID_W
    row_ids = jnp.repeat(jnp.arange(rows, dtype=jnp.float32), GRID_W)
    col_ids = jnp.tile(jnp.arange(GRID_W, dtype=jnp.float32), rows)
    ar = _angles(row_ids, B_AXIS_DIM, AXIAL_THETA)
    ac = _angles(col_ids, B_AXIS_DIM, AXIAL_THETA)
    z = jnp.zeros_like(ar)
    c = jnp.concatenate([jnp.cos(ar), jnp.cos(ar), jnp.cos(ac), jnp.cos(ac)], axis=1)
    sa = jnp.concatenate([-jnp.sin(ar), z, -jnp.sin(ac), z], axis=1)
    sb = jnp.concatenate([z, jnp.sin(ar), z, jnp.sin(ac)], axis=1)
    return c, sa, sb


def _rotate(y, c, sa, sb, shift):
    return (y * c + pltpu.roll(y, LANES - shift, axis=1) * sa
            + pltpu.roll(y, shift, axis=1) * sb)


def _silu(g):
    return g / (1.0 + jnp.exp(-g))


def _rms_scale(y):
    return lax.rsqrt(jnp.mean(y * y, axis=-1, keepdims=True) + NORM_EPS)


def _pre_norm(x, nw_ref):
    return (x * _rms_scale(x) * nw_ref[...]).astype(jnp.bfloat16)


def _inproj_kernel(x_ref, nw_ref, w_ref, c_ref, sa_ref, sb_ref, qn_ref, kn_ref,
                   o_ref, *, mixer):
    _inproj_rows(_pre_norm(x_ref[...], nw_ref), w_ref, c_ref, sa_ref, sb_ref,
                 qn_ref, kn_ref, o_ref, mixer)


def _inproj_rows(h, w_ref, c_ref, sa_ref, sb_ref, qn_ref, kn_ref, o_ref, mixer):
    sub = PROJ_SUB_ROWS[mixer]
    for r0 in range(0, h.shape[0], sub):
        rws = slice(r0, r0 + sub)
        _inproj_sub_rows(h[rws], w_ref, c_ref[rws, :], sa_ref[rws, :], sb_ref[rws, :],
                         qn_ref, kn_ref, o_ref, rws, mixer)


def _inproj_sub_rows(h, w_ref, c, sa, sb, qn_ref, kn_ref, o_ref, rws, mixer):
    n_out = o_ref.shape[1]
    if mixer == "a":
        n_q, n_k, shift = A_HEADS, A_HEADS, A_ROT_DIM // 2
        q_scale = (A_QK_DIM ** -0.5) * LOG2E
        n_plain = A_HEADS
    else:
        n_q, n_k, shift = B_HEADS, B_KV_HEADS, B_AXIS_DIM // 2
        q_scale = (B_HEAD_DIM ** -0.5) * LOG2E
        n_plain = B_KV_HEADS
    for col0 in range(0, n_out, PROJ_COLS):
        y = jnp.dot(h, w_ref[:, col0:col0 + PROJ_COLS],
                    preferred_element_type=jnp.float32)
        for t in range(PROJ_COLS // LANES):
            tile = col0 // LANES + t
            yt = y[:, t * LANES:(t + 1) * LANES]
            if tile < n_q + n_k:
                is_q = tile < n_q
                if mixer == "b":
                    yt = yt * _rms_scale(yt) * (qn_ref[...] if is_q else kn_ref[...])
                yt = _rotate(yt, c, sa, sb, shift)
                if is_q:
                    yt = yt * q_scale
            elif tile >= n_q + n_k + n_plain:
                yt = _silu(yt)
            o_ref[rws, tile * LANES:(tile + 1) * LANES] = yt.astype(o_ref.dtype)


def _inproj(x2, seq, norm_w, w_in, tables, qn, kn, mixer):
    n_tok = x2.shape[0]
    n_out = w_in.shape[1]
    rows = FIRST_PROJ_ROWS
    per_seq = seq // rows
    tab_spec = pl.BlockSpec((rows, LANES), lambda i: (i % per_seq, 0))
    vec_spec = pl.BlockSpec((1, LANES), lambda i: (0, 0))
    return pl.pallas_call(
        functools.partial(_inproj_kernel, mixer=mixer),
        grid=(n_tok // rows,),
        in_specs=[
            pl.BlockSpec((rows, D_MODEL), lambda i: (i, 0)),
            pl.BlockSpec((1, D_MODEL), lambda i: (0, 0)),
            pl.BlockSpec((D_MODEL, n_out), lambda i: (0, 0)),
            tab_spec, tab_spec, tab_spec, vec_spec, vec_spec,
        ],
        out_specs=pl.BlockSpec((rows, n_out), lambda i: (i, 0)),
        out_shape=jax.ShapeDtypeStruct((n_tok, n_out), jnp.bfloat16),
        compiler_params=_compiler_params(("parallel",)),
        name=f"inproj_{mixer}",
    )(x2, norm_w.reshape(1, D_MODEL), w_in, *tables,
      qn.reshape(1, LANES), kn.reshape(1, LANES))


def _lane_fold(x, op):
    out = x[:, 0:LANES]
    for t in range(1, x.shape[1] // LANES):
        out = op(out, x[:, t * LANES:(t + 1) * LANES])
    return out


def _attention_groups(make_qs, finish, n_groups, k_ref, v_ref, vaug_scr, acc_scr, bufs):
    n_chunks, _, rows = bufs[0][0].shape
    for c in range(n_chunks):
        keys = slice(c * KEY_CHUNK, (c + 1) * KEY_CHUNK)
        vaug_scr[0:LANES, keys] = v_ref[keys, :].astype(jnp.float32).T.astype(vaug_scr.dtype)
    vaug_scr[LANES:V_ROWS, :] = jnp.ones((V_ROWS - LANES, v_ref.shape[0]), vaug_scr.dtype)

    def score_chunk(qs, c, s_scr, m_run):
        s = jnp.dot(k_ref[c * KEY_CHUNK:(c + 1) * KEY_CHUNK, :], qs,
                    preferred_element_type=jnp.float32)
        s_scr[c] = s
        m_c = jnp.max(s.reshape(KEY_CHUNK // SUBLANES, SUBLANES, rows), axis=0)
        return m_c if m_run is None else jnp.maximum(m_run, m_c)

    def weigh_chunk(c, s_scr, m_cur, acc):
        p = jnp.exp2(s_scr[c] - m_cur).astype(jnp.bfloat16)
        pv = jnp.dot(vaug_scr[:, c * KEY_CHUNK:(c + 1) * KEY_CHUNK], p,
                     preferred_element_type=jnp.float32)
        return pv if acc is None else acc + pv

    def stage(g_cur, g_next, parity, defer_finish=False):
        s_cur, m_ref = bufs[parity]
        s_next, m_next = bufs[1 - parity]
        if g_next is not None:
            qs_next = make_qs(g_next)
        if g_cur is not None:
            m_cur = m_ref[0:1, :]
        m_run, acc = None, None
        for c in range(n_chunks):
            if g_next is not None:
                m_run = score_chunk(qs_next, c, s_next, m_run)
            if g_cur is not None:
                acc = weigh_chunk(c, s_cur, m_cur, acc)
        if g_next is not None:
            m = jnp.max(m_run, axis=0, keepdims=True)
            m_next[...] = jnp.broadcast_to(m, (SUBLANES, rows))
        if g_cur is not None:
            if defer_finish:
                acc_scr[...] = acc
            else:
                finish(g_cur, acc)

    stage(None, 0, 1)
    stage(0, 1, 0, defer_finish=True)

    def stages(g, count):
        finish(g, acc_scr[...])
        for u in range(1, count + 1):
            stage(g + u, g + u + 1, u % 2, defer_finish=(u == count))

    def body(j, carry):
        stages(GROUP_UNROLL * j, GROUP_UNROLL)
        return carry

    n_loop = (n_groups - 2) // GROUP_UNROLL
    lax.fori_loop(0, n_loop, body, 0)
    done = GROUP_UNROLL * n_loop
    if n_groups - 2 > done:
        stages(done, n_groups - 2 - done)
    finish(n_groups - 2, acc_scr[...])
    stage(n_groups - 1, None, 1)


def _transpose_bf16(x):
    return x.astype(jnp.float32).T.astype(jnp.bfloat16)


def _row_start(g, bq):
    return g * bq if isinstance(g, int) else pl.multiple_of(g * bq, bq)


def _attn_a_kernel(lam_ref, subln_ref, q_ref, k_ref, v_ref, g_ref, o_ref,
                   vaug_scr, acc_scr, s0_scr, m0_scr, s1_scr, m1_scr, *, lam_init):
    bq = ATTN_ROWS // 2
    n_groups = q_ref.shape[0] // bq
    lam = lam_ref[...]
    lam_full = (jnp.exp(jnp.sum(lam[0:1] * lam[1:2], axis=-1, keepdims=True))
                - jnp.exp(jnp.sum(lam[2:3] * lam[3:4], axis=-1, keepdims=True))
                + lam_init)
    subln = subln_ref[...] * (1.0 - lam_init)

    def make_qs(g):
        q = q_ref[pl.ds(_row_start(g, bq), bq), :]
        lane = lax.broadcasted_iota(jnp.int32, q.shape, 1)
        zero = jnp.zeros_like(q)
        qs = jnp.concatenate([jnp.where(lane < A_QK_DIM, q, zero),
                              jnp.where(lane >= A_QK_DIM, q, zero)], axis=0)
        return _transpose_bf16(qs)

    def finish(g, acc):
        r0 = _row_start(g, bq)
        o = acc[0:LANES, :] / acc[LANES:LANES + 1, :]
        o = (o[:, :bq] - lam_full * o[:, bq:]).T
        o = o * _rms_scale(o) * subln
        gate = g_ref[pl.ds(r0, bq), :].astype(jnp.float32)
        o_ref[pl.ds(r0, bq), :] = (o * gate).astype(o_ref.dtype)

    _attention_groups(make_qs, finish, n_groups, k_ref, v_ref, vaug_scr, acc_scr,
                      ((s0_scr, m0_scr), (s1_scr, m1_scr)))


def _attn_scratch(seq):
    n_chunks = seq // KEY_CHUNK
    scores = pltpu.VMEM((n_chunks, KEY_CHUNK, ATTN_ROWS), jnp.float32)
    maxima = pltpu.VMEM((SUBLANES, ATTN_ROWS), jnp.float32)
    return [pltpu.VMEM((V_ROWS, seq), jnp.bfloat16),
            pltpu.VMEM((V_ROWS, ATTN_ROWS), jnp.float32),
            scores, maxima, scores, maxima]


def _attn_a(proj, batch, seq, lam, subln, lam_init):
    head_spec = lambda tile0: pl.BlockSpec((seq, LANES), lambda b, h: (b, tile0 + h))
    return pl.pallas_call(
        functools.partial(_attn_a_kernel, lam_init=lam_init),
        grid=(batch, A_HEADS),
        in_specs=[
            pl.BlockSpec((4, A_QK_DIM), lambda b, h: (0, 0)),
            pl.BlockSpec((1, LANES), lambda b, h: (0, 0)),
            head_spec(0), head_spec(A_HEADS), head_spec(2 * A_HEADS), head_spec(3 * A_HEADS),
        ],
        out_specs=head_spec(0),
        out_shape=jax.ShapeDtypeStruct((batch * seq, A_HEADS * LANES), jnp.bfloat16),
        scratch_shapes=_attn_scratch(seq),
        compiler_params=_compiler_params(("parallel", "parallel")),
        name="attn_a",
    )(lam, subln.reshape(1, LANES), proj, proj, proj, proj)


def _attn_b_kernel(q_ref, k_ref, v_ref, g_ref, o_ref,
                   vaug_scr, acc_scr, s0_scr, m0_scr, s1_scr, m1_scr):
    bq = ATTN_ROWS // B_GROUP
    n_groups = q_ref.shape[0] // bq

    def make_qs(g):
        r0 = _row_start(g, bq)
        return _transpose_bf16(jnp.concatenate(
            [q_ref[pl.ds(r0, bq), i * LANES:(i + 1) * LANES] for i in range(B_GROUP)], axis=0))

    def finish(g, acc):
        r0 = _row_start(g, bq)
        o = (acc[0:LANES, :] / acc[LANES:LANES + 1, :]).T
        for i in range(B_GROUP):
            gate = g_ref[pl.ds(r0, bq), i * LANES:(i + 1) * LANES].astype(jnp.float32)
            o_ref[pl.ds(r0, bq), i * LANES:(i + 1) * LANES] = (
                o[i * bq:(i + 1) * bq] * gate).astype(o_ref.dtype)

    _attention_groups(make_qs, finish, n_groups, k_ref, v_ref, vaug_scr, acc_scr,
                      ((s0_scr, m0_scr), (s1_scr, m1_scr)))


def _attn_b(proj, batch, seq):
    q_rows = min(seq, B_SPLIT_ROWS)
    n_split = seq // q_rows
    group_cols = B_GROUP * LANES
    k_tile0 = B_HEADS
    v_tile0 = B_HEADS + B_KV_HEADS
    g_blk0 = (B_HEADS + 2 * B_KV_HEADS) * LANES // group_cols
    q_spec = lambda blk0: pl.BlockSpec(
        (q_rows, group_cols), lambda b, kv, i: (b * n_split + i, blk0 + kv))
    kv_spec = lambda tile0: pl.BlockSpec((seq, LANES), lambda b, kv, i: (b, tile0 + kv))
    return pl.pallas_call(
        _attn_b_kernel,
        grid=(batch, B_KV_HEADS, n_split),
        in_specs=[q_spec(0), kv_spec(k_tile0), kv_spec(v_tile0), q_spec(g_blk0)],
        out_specs=q_spec(0),
        out_shape=jax.ShapeDtypeStruct((batch * seq, B_HEADS * LANES), jnp.bfloat16),
        scratch_shapes=_attn_scratch(seq),
        compiler_params=_compiler_params(("parallel", "parallel", "parallel")),
        name="attn_b",
    )(proj, proj, proj, proj)


def _outproj_rows(o_ref, w_ref, nw_ref, x_ref):
    m = jnp.dot(o_ref[...], w_ref[...], preferred_element_type=jnp.float32)
    return x_ref[...] + m * _rms_scale(m) * nw_ref[...]


def _outproj_kernel(o_ref, w_ref, nw_ref, x_ref, y_ref):
    y_ref[...] = _outproj_rows(o_ref, w_ref, nw_ref, x_ref)


def _out_in_kernel(o_ref, wo_ref, nwo_ref, x_ref, nwi_ref, wi_ref, c_ref, sa_ref, sb_ref,
                   qn_ref, kn_ref, y_ref, p_ref, h_scr, *, mixer):
    @pl.when(pl.program_id(0) == 0)
    def _():
        h_scr[...] = jnp.zeros_like(h_scr)

    h_prev = h_scr[...]
    y = _outproj_rows(o_ref, wo_ref, nwo_ref, x_ref)
    y_ref[...] = y
    _inproj_rows(h_prev, wi_ref, c_ref, sa_ref, sb_ref, qn_ref, kn_ref, p_ref, mixer)
    h_scr[...] = _pre_norm(y, nwi_ref)


def _out_in(o, w_out, norm_post, x2, seq, norm_pre, w_in, tables, qn, kn, mixer):
    n_tok = x2.shape[0]
    n_out = w_in.shape[1]
    rows = PROJ_ROWS[mixer]
    n_tiles = n_tok // rows
    per_seq = seq // rows
    cur = lambda i: (jnp.minimum(i, n_tiles - 1), 0)
    prev = lambda i: (jnp.maximum(i - 1, 0), 0)
    const = lambda shape: pl.BlockSpec(shape, lambda i: (0, 0))
    tab_spec = pl.BlockSpec((rows, LANES), lambda i: (jnp.maximum(i - 1, 0) % per_seq, 0))
    return pl.pallas_call(
        functools.partial(_out_in_kernel, mixer=mixer),
        grid=(n_tiles + 1,),
        in_specs=[
            pl.BlockSpec((rows, D_MODEL), cur), const((D_MODEL, D_MODEL)),
            const((1, D_MODEL)), pl.BlockSpec((rows, D_MODEL), cur),
            const((1, D_MODEL)), const((D_MODEL, n_out)),
            tab_spec, tab_spec, tab_spec, const((1, LANES)), const((1, LANES)),
        ],
        out_specs=[pl.BlockSpec((rows, D_MODEL), cur), pl.BlockSpec((rows, n_out), prev)],
        out_shape=[jax.ShapeDtypeStruct((n_tok, D_MODEL), jnp.float32),
                   jax.ShapeDtypeStruct((n_tok, n_out), jnp.bfloat16)],
        scratch_shapes=[pltpu.VMEM((rows, D_MODEL), jnp.bfloat16)],
        compiler_params=_compiler_params(("arbitrary",)),
        name=f"out_in_{mixer}",
    )(o, w_out, norm_post.reshape(1, D_MODEL), x2, norm_pre.reshape(1, D_MODEL), w_in,
      *tables, qn.reshape(1, LANES), kn.reshape(1, LANES))


def _outproj(o, w_out, norm_w, x2):
    n_tok = x2.shape[0]
    row_spec = pl.BlockSpec((OUT_ROWS, D_MODEL), lambda i: (i, 0))
    return pl.pallas_call(
        _outproj_kernel,
        grid=(n_tok // OUT_ROWS,),
        in_specs=[
            row_spec,
            pl.BlockSpec((D_MODEL, D_MODEL), lambda i: (0, 0)),
            pl.BlockSpec((1, D_MODEL), lambda i: (0, 0)),
            row_spec,
        ],
        out_specs=row_spec,
        out_shape=jax.ShapeDtypeStruct((n_tok, D_MODEL), jnp.float32),
        compiler_params=_compiler_params(("parallel",)),
        name="outproj",
    )(o, w_out, norm_w.reshape(1, D_MODEL), x2)


def _trunk(x, norm_pre, norm_post, a_w_in, a_w_out, a_lam, a_subln,
           b_w_in, b_w_out, b_q_norm, b_k_norm):
    batch, seq, _ = x.shape
    x2 = x.reshape(batch * seq, D_MODEL)
    tables_a = _rope_tables_a(seq)
    tables_b = _rope_tables_b(seq)
    ones = jnp.ones((LANES,), jnp.float32)

    def inproj_args(i):
        j = i // 2
        if i % 2 == 0:
            return (seq, norm_pre[i], a_w_in[j], tables_a, ones, ones, "a")
        return (seq, norm_pre[i], b_w_in[j], tables_b, b_q_norm[j], b_k_norm[j], "b")

    proj = _inproj(x2, *inproj_args(0))
    for i in range(DEPTH):
        j = i // 2
        if i % 2 == 0:
            lam_init = 0.8 - 0.6 * math.exp(-0.3 * i)
            o = _attn_a(proj, batch, seq, a_lam[j], a_subln[j], lam_init)
            w_out = a_w_out[j]
        else:
            o = _attn_b(proj, batch, seq)
            w_out = b_w_out[j]
        if i + 1 < DEPTH:
            x2, proj = _out_in(o, w_out, norm_post[i], x2, *inproj_args(i + 1))
        else:
            x2 = _outproj(o, w_out, norm_post[i], x2)
    return x2.reshape(batch, seq, D_MODEL)


def kernel(x_prompt, x_sample, norm_pre, norm_post, a_w_in, a_w_out, a_lam, a_subln,
           b_w_in, b_w_out, b_q_norm, b_k_norm):
    bf16 = jnp.bfloat16
    weights = (norm_pre, norm_post, a_w_in.astype(bf16), a_w_out.astype(bf16), a_lam,
               a_subln, b_w_in.astype(bf16), b_w_out.astype(bf16), b_q_norm, b_k_norm)
    return (_trunk(x_prompt, *weights), _trunk(x_sample, *weights))
```

```python
import functools
import math

import jax
import jax.numpy as jnp
from jax import lax
from jax.experimental import pallas as pl
from jax.experimental.pallas import tpu as pltpu

D_MODEL = 1024
DEPTH = 4
NORM_EPS = 1e-6

A_HEADS = 8
A_QK_DIM = 64
A_ROT_DIM = 16
ROPE_THETA = 500000.0

B_HEADS = 8
B_KV_HEADS = 2
B_GROUP = B_HEADS // B_KV_HEADS
B_HEAD_DIM = 128
B_AXIS_DIM = 64
AXIAL_THETA = 10000.0
GRID_W = 64

LANES = 128
SUBLANES = 8
V_ROWS = LANES + 16
LOG2E = 1.4426950408889634
VMEM_LIMIT_BYTES = 56 * 1024 * 1024

PROJ_ROWS = {"a": 512, "b": 512}
PROJ_SUB_ROWS = {"a": 512, "b": 256}
FIRST_PROJ_ROWS = 1024
OUT_ROWS = 1024
PROJ_COLS = 512
ATTN_ROWS = 512
KEY_CHUNK = 512
GROUP_UNROLL = 6
B_SPLIT_ROWS = 2048


def _compiler_params(semantics):
    return pltpu.CompilerParams(
        dimension_semantics=semantics, vmem_limit_bytes=VMEM_LIMIT_BYTES)


def _angles(pos, dim, theta):
    inv_freq = theta ** (-(jnp.arange(0, dim, 2, dtype=jnp.float32) / dim))
    return pos[:, None] * inv_freq[None, :]


def _rope_tables_a(seq):
    ang = _angles(jnp.arange(seq, dtype=jnp.float32), A_ROT_DIM, ROPE_THETA)
    cos, sin = jnp.cos(ang), jnp.sin(ang)
    half = A_ROT_DIM // 2
    one = jnp.ones((seq, A_QK_DIM - A_ROT_DIM), jnp.float32)
    zero_h = jnp.zeros((seq, half), jnp.float32)
    zero_r = jnp.zeros((seq, A_QK_DIM - A_ROT_DIM), jnp.float32)
    c = jnp.concatenate([cos, cos, one], axis=1)
    sa = jnp.concatenate([-sin, zero_h, zero_r], axis=1)
    sb = jnp.concatenate([zero_h, sin, zero_r], axis=1)
    tile2 = lambda t: jnp.concatenate([t, t], axis=1)
    return tile2(c), tile2(sa), tile2(sb)


def _rope_tables_b(seq):
    rows = seq // GRID_W
    row_ids = jnp.repeat(jnp.arange(rows, dtype=jnp.float32), GRID_W)
    col_ids = jnp.tile(jnp.arange(GRID_W, dtype=jnp.float32), rows)
    ar = _angles(row_ids, B_AXIS_DIM, AXIAL_THETA)
    ac = _angles(col_ids, B_AXIS_DIM, AXIAL_THETA)
    z = jnp.zeros_like(ar)
    c = jnp.concatenate([jnp.cos(ar), jnp.cos(ar), jnp.cos(ac), jnp.cos(ac)], axis=1)
    sa = jnp.concatenate([-jnp.sin(ar), z, -jnp.sin(ac), z], axis=1)
    sb = jnp.concatenate([z, jnp.sin(ar), z, jnp.sin(ac)], axis=1)
    return c, sa, sb


def _rotate(y, c, sa, sb, shift):
    return (y * c + pltpu.roll(y, LANES - shift, axis=1) * sa
            + pltpu.roll(y, shift, axis=1) * sb)


def _silu(g):
    return g / (1.0 + jnp.exp(-g))


def _rms_scale(y):
    return lax.rsqrt(jnp.mean(y * y, axis=-1, keepdims=True) + NORM_EPS)


def _pre_norm(x, nw_ref):
    return (x * _rms_scale(x) * nw_ref[...]).astype(jnp.bfloat16)


def _inproj_kernel(x_ref, nw_ref, w_ref, c_ref, sa_ref, sb_ref, qn_ref, kn_ref,
                   o_ref, *, mixer):
    _inproj_rows(_pre_norm(x_ref[...], nw_ref), w_ref, c_ref, sa_ref, sb_ref,
                 qn_ref, kn_ref, o_ref, mixer)


def _inproj_rows(h, w_ref, c_ref, sa_ref, sb_ref, qn_ref, kn_ref, o_ref, mixer):
    sub = PROJ_SUB_ROWS[mixer]
    for r0 in range(0, h.shape[0], sub):
        rws = slice(r0, r0 + sub)
        _inproj_sub_rows(h[rws], w_ref, c_ref[rws, :], sa_ref[rws, :], sb_ref[rws, :],
                         qn_ref, kn_ref, o_ref, rws, mixer)


def _inproj_sub_rows(h, w_ref, c, sa, sb, qn_ref, kn_ref, o_ref, rws, mixer):
    n_out = o_ref.shape[1]
    if mixer == "a":
        n_q, n_k, shift = A_HEADS, A_HEADS, A_ROT_DIM // 2
        q_scale = (A_QK_DIM ** -0.5) * LOG2E
        n_plain = A_HEADS
    else:
        n_q, n_k, shift = B_HEADS, B_KV_HEADS, B_AXIS_DIM // 2
        q_scale = (B_HEAD_DIM ** -0.5) * LOG2E
        n_plain = B_KV_HEADS
    for col0 in range(0, n_out, PROJ_COLS):
        y = jnp.dot(h, w_ref[:, col0:col0 + PROJ_COLS],
                    preferred_element_type=jnp.float32)
        for t in range(PROJ_COLS // LANES):
            tile = col0 // LANES + t
            yt = y[:, t * LANES:(t + 1) * LANES]
            if tile < n_q + n_k:
                is_q = tile < n_q
                if mixer == "b":
                    yt = yt * _rms_scale(yt) * (qn_ref[...] if is_q else kn_ref[...])
                yt = _rotate(yt, c, sa, sb, shift)
                if is_q:
                    yt = yt * q_scale
            elif tile >= n_q + n_k + n_plain:
                yt = _silu(yt)
            o_ref[rws, tile * LANES:(tile + 1) * LANES] = yt.astype(o_ref.dtype)


def _inproj(x2, seq, norm_w, w_in, tables, qn, kn, mixer):
    n_tok = x2.shape[0]
    n_out = w_in.shape[1]
    rows = FIRST_PROJ_ROWS
    per_seq = seq // rows
    tab_spec = pl.BlockSpec((rows, LANES), lambda i: (i % per_seq, 0))
    vec_spec = pl.BlockSpec((1, LANES), lambda i: (0, 0))
    return pl.pallas_call(
        functools.partial(_inproj_kernel, mixer=mixer),
        grid=(n_tok // rows,),
        in_specs=[
            pl.BlockSpec((rows, D_MODEL), lambda i: (i, 0)),
            pl.BlockSpec((1, D_MODEL), lambda i: (0, 0)),
            pl.BlockSpec((D_MODEL, n_out), lambda i: (0, 0)),
            tab_spec, tab_spec, tab_spec, vec_spec, vec_spec,
        ],
        out_specs=pl.BlockSpec((rows, n_out), lambda i: (i, 0)),
        out_shape=jax.ShapeDtypeStruct((n_tok, n_out), jnp.bfloat16),
        compiler_params=_compiler_params(("parallel",)),
        name=f"inproj_{mixer}",
    )(x2, norm_w.reshape(1, D_MODEL), w_in, *tables,
      qn.reshape(1, LANES), kn.reshape(1, LANES))


def _lane_fold(x, op):
    out = x[:, 0:LANES]
    for t in range(1, x.shape[1] // LANES):
        out = op(out, x[:, t * LANES:(t + 1) * LANES])
    return out


def _attention_groups(make_qs, finish, n_groups, k_ref, v_ref, vaug_scr, acc_scr, bufs):
    n_chunks, _, rows = bufs[0][0].shape
    for c in range(n_chunks):
        keys = slice(c * KEY_CHUNK, (c + 1) * KEY_CHUNK)
        vaug_scr[0:LANES, keys] = v_ref[keys, :].astype(jnp.float32).T.astype(vaug_scr.dtype)
    vaug_scr[LANES:V_ROWS, :] = jnp.ones((V_ROWS - LANES, v_ref.shape[0]), vaug_scr.dtype)

    def score_chunk(qs, c, s_scr, m_run):
        s = jnp.dot(k_ref[c * KEY_CHUNK:(c + 1) * KEY_CHUNK, :], qs,
                    preferred_element_type=jnp.float32)
        s_scr[c] = s
        m_c = jnp.max(s.reshape(KEY_CHUNK // SUBLANES, SUBLANES, rows), axis=0)
        return m_c if m_run is None else jnp.maximum(m_run, m_c)

    def weigh_chunk(c, s_scr, m_cur, acc):
        p = jnp.exp2(s_scr[c] - m_cur).astype(jnp.bfloat16)
        pv = jnp.dot(vaug_scr[:, c * KEY_CHUNK:(c + 1) * KEY_CHUNK], p,
                     preferred_element_type=jnp.float32)
        return pv if acc is None else acc + pv

    def stage(g_cur, g_next, parity, defer_finish=False):
        s_cur, m_ref = bufs[parity]
        s_next, m_next = bufs[1 - parity]
        if g_next is not None:
            qs_next = make_qs(g_next)
        if g_cur is not None:
            m_cur = m_ref[0:1, :]
        m_run, acc = None, None
        for c in range(n_chunks):
            if g_next is not None:
                m_run = score_chunk(qs_next, c, s_next, m_run)
            if g_cur is not None:
                acc = weigh_chunk(c, s_cur, m_cur, acc)
        if g_next is not None:
            m = jnp.max(m_run, axis=0, keepdims=True)
            m_next[...] = jnp.broadcast_to(m, (SUBLANES, rows))
        if g_cur is not None:
            if defer_finish:
                acc_scr[...] = acc
            else:
                finish(g_cur, acc)

    stage(None, 0, 1)
    stage(0, 1, 0, defer_finish=True)

    def stages(g, count):
        finish(g, acc_scr[...])
        for u in range(1, count + 1):
            stage(g + u, g + u + 1, u % 2, defer_finish=(u == count))

    unroll = GROUP_UNROLL if n_groups - 2 >= 2 * GROUP_UNROLL else 2

    def body(j, carry):
        stages(unroll * j, unroll)
        return carry

    n_loop = (n_groups - 2) // unroll
    lax.fori_loop(0, n_loop, body, 0)
    done = unroll * n_loop
    if n_groups - 2 > done:
        stages(done, n_groups - 2 - done)
    finish(n_groups - 2, acc_scr[...])
    stage(n_groups - 1, None, 1)


def _transpose_bf16(x):
    return x.astype(jnp.float32).T.astype(jnp.bfloat16)


def _row_start(g, bq):
    return g * bq if isinstance(g, int) else pl.multiple_of(g * bq, bq)


def _attn_a_kernel(lam_ref, subln_ref, q_ref, k_ref, v_ref, g_ref, o_ref,
                   vaug_scr, acc_scr, s0_scr, m0_scr, s1_scr, m1_scr, *, lam_init):
    bq = ATTN_ROWS // 2
    n_groups = q_ref.shape[0] // bq
    lam = lam_ref[...]
    lam_full = (jnp.exp(jnp.sum(lam[0:1] * lam[1:2], axis=-1, keepdims=True))
                - jnp.exp(jnp.sum(lam[2:3] * lam[3:4], axis=-1, keepdims=True))
                + lam_init)
    subln = subln_ref[...] * (1.0 - lam_init)

    def make_qs(g):
        q = q_ref[pl.ds(_row_start(g, bq), bq), :]
        lane = lax.broadcasted_iota(jnp.int32, q.shape, 1)
        zero = jnp.zeros_like(q)
        qs = jnp.concatenate([jnp.where(lane < A_QK_DIM, q, zero),
                              jnp.where(lane >= A_QK_DIM, q, zero)], axis=0)
        return _transpose_bf16(qs)

    def finish(g, acc):
        r0 = _row_start(g, bq)
        o = acc[0:LANES, :] / acc[LANES:LANES + 1, :]
        o = (o[:, :bq] - lam_full * o[:, bq:]).T
        o = o * _rms_scale(o) * subln
        gate = g_ref[pl.ds(r0, bq), :].astype(jnp.float32)
        o_ref[pl.ds(r0, bq), :] = (o * gate).astype(o_ref.dtype)

    _attention_groups(make_qs, finish, n_groups, k_ref, v_ref, vaug_scr, acc_scr,
                      ((s0_scr, m0_scr), (s1_scr, m1_scr)))


def _attn_scratch(seq):
    n_chunks = seq // KEY_CHUNK
    scores = pltpu.VMEM((n_chunks, KEY_CHUNK, ATTN_ROWS), jnp.float32)
    maxima = pltpu.VMEM((SUBLANES, ATTN_ROWS), jnp.float32)
    return [pltpu.VMEM((V_ROWS, seq), jnp.bfloat16),
            pltpu.VMEM((V_ROWS, ATTN_ROWS), jnp.float32),
            scores, maxima, scores, maxima]


def _attn_a(proj, batch, seq, lam, subln, lam_init):
    head_spec = lambda tile0: pl.BlockSpec((seq, LANES), lambda b, h: (b, tile0 + h))
    return pl.pallas_call(
        functools.partial(_attn_a_kernel, lam_init=lam_init),
        grid=(batch, A_HEADS),
        in_specs=[
            pl.BlockSpec((4, A_QK_DIM), lambda b, h: (0, 0)),
            pl.BlockSpec((1, LANES), lambda b, h: (0, 0)),
            head_spec(0), head_spec(A_HEADS), head_spec(2 * A_HEADS), head_spec(3 * A_HEADS),
        ],
        out_specs=head_spec(0),
        out_shape=jax.ShapeDtypeStruct((batch * seq, A_HEADS * LANES), jnp.bfloat16),
        scratch_shapes=_attn_scratch(seq),
        compiler_params=_compiler_params(("parallel", "parallel")),
        name="attn_a",
    )(lam, subln.reshape(1, LANES), proj, proj, proj, proj)


def _attn_b_kernel(q_ref, k_ref, v_ref, g_ref, o_ref,
                   vaug_scr, acc_scr, s0_scr, m0_scr, s1_scr, m1_scr):
    bq = ATTN_ROWS // B_GROUP
    n_groups = q_ref.shape[0] // bq

    def make_qs(g):
        r0 = _row_start(g, bq)
        return _transpose_bf16(jnp.concatenate(
            [q_ref[pl.ds(r0, bq), i * LANES:(i + 1) * LANES] for i in range(B_GROUP)], axis=0))

    def finish(g, acc):
        r0 = _row_start(g, bq)
        o = (acc[0:LANES, :] / acc[LANES:LANES + 1, :]).T
        for i in range(B_GROUP):
            gate = g_ref[pl.ds(r0, bq), i * LANES:(i + 1) * LANES].astype(jnp.float32)
            o_ref[pl.ds(r0, bq), i * LANES:(i + 1) * LANES] = (
                o[i * bq:(i + 1) * bq] * gate).astype(o_ref.dtype)

    _attention_groups(make_qs, finish, n_groups, k_ref, v_ref, vaug_scr, acc_scr,
                      ((s0_scr, m0_scr), (s1_scr, m1_scr)))


def _attn_b(proj, batch, seq):
    q_rows = min(seq, B_SPLIT_ROWS)
    n_split = seq // q_rows
    group_cols = B_GROUP * LANES
    k_tile0 = B_HEADS
    v_tile0 = B_HEADS + B_KV_HEADS
    g_blk0 = (B_HEADS + 2 * B_KV_HEADS) * LANES // group_cols
    q_spec = lambda blk0: pl.BlockSpec(
        (q_rows, group_cols), lambda b, kv, i: (b * n_split + i, blk0 + kv))
    kv_spec = lambda tile0: pl.BlockSpec((seq, LANES), lambda b, kv, i: (b, tile0 + kv))
    return pl.pallas_call(
        _attn_b_kernel,
        grid=(batch, B_KV_HEADS, n_split),
        in_specs=[q_spec(0), kv_spec(k_tile0), kv_spec(v_tile0), q_spec(g_blk0)],
        out_specs=q_spec(0),
        out_shape=jax.ShapeDtypeStruct((batch * seq, B_HEADS * LANES), jnp.bfloat16),
        scratch_shapes=_attn_scratch(seq),
        compiler_params=_compiler_params(("parallel", "parallel", "parallel")),
        name="attn_b",
    )(proj, proj, proj, proj)


def _outproj_rows(o_ref, w_ref, nw_ref, x_ref):
    m = jnp.dot(o_ref[...], w_ref[...], preferred_element_type=jnp.float32)
    return x_ref[...] + m * _rms_scale(m) * nw_ref[...]


def _outproj_kernel(o_ref, w_ref, nw_ref, x_ref, y_ref):
    y_ref[...] = _outproj_rows(o_ref, w_ref, nw_ref, x_ref)


def _out_in_kernel(o_ref, wo_ref, nwo_ref, x_ref, nwi_ref, wi_ref, c_ref, sa_ref, sb_ref,
                   qn_ref, kn_ref, y_ref, p_ref, h_scr, *, mixer):
    @pl.when(pl.program_id(0) == 0)
    def _():
        h_scr[...] = jnp.zeros_like(h_scr)

    h_prev = h_scr[...]
    y = _outproj_rows(o_ref, wo_ref, nwo_ref, x_ref)
    y_ref[...] = y
    _inproj_rows(h_prev, wi_ref, c_ref, sa_ref, sb_ref, qn_ref, kn_ref, p_ref, mixer)
    h_scr[...] = _pre_norm(y, nwi_ref)


def _out_in(o, w_out, norm_post, x2, seq, norm_pre, w_in, tables, qn, kn, mixer):
    n_tok = x2.shape[0]
    n_out = w_in.shape[1]
    rows = PROJ_ROWS[mixer]
    n_tiles = n_tok // rows
    per_seq = seq // rows
    cur = lambda i: (jnp.minimum(i, n_tiles - 1), 0)
    prev = lambda i: (jnp.maximum(i - 1, 0), 0)
    const = lambda shape: pl.BlockSpec(shape, lambda i: (0, 0))
    tab_spec = pl.BlockSpec((rows, LANES), lambda i: (jnp.maximum(i - 1, 0) % per_seq, 0))
    return pl.pallas_call(
        functools.partial(_out_in_kernel, mixer=mixer),
        grid=(n_tiles + 1,),
        in_specs=[
            pl.BlockSpec((rows, D_MODEL), cur), const((D_MODEL, D_MODEL)),
            const((1, D_MODEL)), pl.BlockSpec((rows, D_MODEL), cur),
            const((1, D_MODEL)), const((D_MODEL, n_out)),
            tab_spec, tab_spec, tab_spec, const((1, LANES)), const((1, LANES)),
        ],
        out_specs=[pl.BlockSpec((rows, D_MODEL), cur), pl.BlockSpec((rows, n_out), prev)],
        out_shape=[jax.ShapeDtypeStruct((n_tok, D_MODEL), jnp.float32),
                   jax.ShapeDtypeStruct((n_tok, n_out), jnp.bfloat16)],
        scratch_shapes=[pltpu.VMEM((rows, D_MODEL), jnp.bfloat16)],
        compiler_params=_compiler_params(("arbitrary",)),
        name=f"out_in_{mixer}",
    )(o, w_out, norm_post.reshape(1, D_MODEL), x2, norm_pre.reshape(1, D_MODEL), w_in,
      *tables, qn.reshape(1, LANES), kn.reshape(1, LANES))


def _outproj(o, w_out, norm_w, x2):
    n_tok = x2.shape[0]
    row_spec = pl.BlockSpec((OUT_ROWS, D_MODEL), lambda i: (i, 0))
    return pl.pallas_call(
        _outproj_kernel,
        grid=(n_tok // OUT_ROWS,),
        in_specs=[
            row_spec,
            pl.BlockSpec((D_MODEL, D_MODEL), lambda i: (0, 0)),
            pl.BlockSpec((1, D_MODEL), lambda i: (0, 0)),
            row_spec,
        ],
        out_specs=row_spec,
        out_shape=jax.ShapeDtypeStruct((n_tok, D_MODEL), jnp.float32),
        compiler_params=_compiler_params(("parallel",)),
        name="outproj",
    )(o, w_out, norm_w.reshape(1, D_MODEL), x2)


def _trunk(x, norm_pre, norm_post, a_w_in, a_w_out, a_lam, a_subln,
           b_w_in, b_w_out, b_q_norm, b_k_norm):
    batch, seq, _ = x.shape
    x2 = x.reshape(batch * seq, D_MODEL)
    tables_a = _rope_tables_a(seq)
    tables_b = _rope_tables_b(seq)
    ones = jnp.ones((LANES,), jnp.float32)

    def inproj_args(i):
        j = i // 2
        if i % 2 == 0:
            return (seq, norm_pre[i], a_w_in[j], tables_a, ones, ones, "a")
        return (seq, norm_pre[i], b_w_in[j], tables_b, b_q_norm[j], b_k_norm[j], "b")

    proj = _inproj(x2, *inproj_args(0))
    for i in range(DEPTH):
        j = i // 2
        if i % 2 == 0:
            lam_init = 0.8 - 0.6 * math.exp(-0.3 * i)
            o = _attn_a(proj, batch, seq, a_lam[j], a_subln[j], lam_init)
            w_out = a_w_out[j]
        else:
            o = _attn_b(proj, batch, seq)
            w_out = b_w_out[j]
        if i + 1 < DEPTH:
            x2, proj = _out_in(o, w_out, norm_post[i], x2, *inproj_args(i + 1))
        else:
            x2 = _outproj(o, w_out, norm_post[i], x2)
    return x2.reshape(batch, seq, D_MODEL)


def kernel(x_prompt, x_sample, norm_pre, norm_post, a_w_in, a_w_out, a_lam, a_subln,
           b_w_in, b_w_out, b_q_norm, b_k_norm):
    bf16 = jnp.bfloat16
    weights = (norm_pre, norm_post, a_w_in.astype(bf16), a_w_out.astype(bf16), a_lam,
               a_subln, b_w_in.astype(bf16), b_w_out.astype(bf16), b_q_norm, b_k_norm)
    return (_trunk(x_prompt, *weights), _trunk(x_sample, *weights))
```

```python
import functools
import math

import jax
import jax.numpy as jnp
from jax import lax
from jax.experimental import pallas as pl
from jax.experimental.pallas import tpu as pltpu

D_MODEL = 1024
DEPTH = 4
NORM_EPS = 1e-6

A_HEADS = 8
A_QK_DIM = 64
A_ROT_DIM = 16
ROPE_THETA = 500000.0

B_HEADS = 8
B_KV_HEADS = 2
B_GROUP = B_HEADS // B_KV_HEADS
B_HEAD_DIM = 128
B_AXIS_DIM = 64
AXIAL_THETA = 10000.0
GRID_W = 64

LANES = 128
SUBLANES = 8
V_ROWS = LANES + 16
LOG2E = 1.4426950408889634
VMEM_LIMIT_BYTES = 56 * 1024 * 1024

PROJ_ROWS = {"a": 512, "b": 512}
PROJ_SUB_ROWS = {"a": 512, "b": 256}
FIRST_PROJ_ROWS = 1024
OUT_ROWS = 1024
PROJ_COLS = 512
ATTN_ROWS = 512
KEY_CHUNK = 512
GROUP_UNROLL = 6
B_SPLIT_ROWS = 4096


def _compiler_params(semantics):
    return pltpu.CompilerParams(
        dimension_semantics=semantics, vmem_limit_bytes=VMEM_LIMIT_BYTES)


def _angles(pos, dim, theta):
    inv_freq = theta ** (-(jnp.arange(0, dim, 2, dtype=jnp.float32) / dim))
    return pos[:, None] * inv_freq[None, :]


def _rope_tables_a(seq):
    ang = _angles(jnp.arange(seq, dtype=jnp.float32), A_ROT_DIM, ROPE_THETA)
    cos, sin = jnp.cos(ang), jnp.sin(ang)
    half = A_ROT_DIM // 2
    one = jnp.ones((seq, A_QK_DIM - A_ROT_DIM), jnp.float32)
    zero_h = jnp.zeros((seq, half), jnp.float32)
    zero_r = jnp.zeros((seq, A_QK_DIM - A_ROT_DIM), jnp.float32)
    c = jnp.concatenate([cos, cos, one], axis=1)
    sa = jnp.concatenate([-sin, zero_h, zero_r], axis=1)
    sb = jnp.concatenate([zero_h, sin, zero_r], axis=1)
    tile2 = lambda t: jnp.concatenate([t, t], axis=1)
    return tile2(c), tile2(sa), tile2(sb)


def _rope_tables_b(seq):
    rows = seq // GRID_W
    row_ids = jnp.repeat(jnp.arange(rows, dtype=jnp.float32), GRID_W)
    col_ids = jnp.tile(jnp.arange(GRID_W, dtype=jnp.float32), rows)
    ar = _angles(row_ids, B_AXIS_DIM, AXIAL_THETA)
    ac = _angles(col_ids, B_AXIS_DIM, AXIAL_THETA)
    z = jnp.zeros_like(ar)
    c = jnp.concatenate([jnp.cos(ar), jnp.cos(ar), jnp.cos(ac), jnp.cos(ac)], axis=1)
    sa = jnp.concatenate([-jnp.sin(ar), z, -jnp.sin(ac), z], axis=1)
    sb = jnp.concatenate([z, jnp.sin(ar), z, jnp.sin(ac)], axis=1)
    return c, sa, sb


def _rotate(y, c, sa, sb, shift):
    return (y * c + pltpu.roll(y, LANES - shift, axis=1) * sa
            + pltpu.roll(y, shift, axis=1) * sb)


def _silu(g):
    return g / (1.0 + jnp.exp(-g))


def _rms_scale(y):
    return lax.rsqrt(jnp.mean(y * y, axis=-1, keepdims=True) + NORM_EPS)


def _pre_norm(x, nw_ref):
    return (x * _rms_scale(x) * nw_ref[...]).astype(jnp.bfloat16)


def _inproj_kernel(x_ref, nw_ref, w_ref, c_ref, sa_ref, sb_ref, qn_ref, kn_ref,
                   o_ref, *, mixer):
    _inproj_rows(_pre_norm(x_ref[...], nw_ref), w_ref, c_ref, sa_ref, sb_ref,
                 qn_ref, kn_ref, o_ref, mixer)


def _inproj_rows(h, w_ref, c_ref, sa_ref, sb_ref, qn_ref, kn_ref, o_ref, mixer):
    sub = PROJ_SUB_ROWS[mixer]
    for r0 in range(0, h.shape[0], sub):
        rws = slice(r0, r0 + sub)
        _inproj_sub_rows(h[rws], w_ref, c_ref[rws, :], sa_ref[rws, :], sb_ref[rws, :],
                         qn_ref, kn_ref, o_ref, rws, mixer)


def _inproj_sub_rows(h, w_ref, c, sa, sb, qn_ref, kn_ref, o_ref, rws, mixer):
    n_out = o_ref.shape[1]
    if mixer == "a":
        n_q, n_k, shift = A_HEADS, A_HEADS, A_ROT_DIM // 2
        q_scale = (A_QK_DIM ** -0.5) * LOG2E
        n_plain = A_HEADS
    else:
        n_q, n_k, shift = B_HEADS, B_KV_HEADS, B_AXIS_DIM // 2
        q_scale = (B_HEAD_DIM ** -0.5) * LOG2E
        n_plain = B_KV_HEADS
    for col0 in range(0, n_out, PROJ_COLS):
        y = jnp.dot(h, w_ref[:, col0:col0 + PROJ_COLS],
                    preferred_element_type=jnp.float32)
        for t in range(PROJ_COLS // LANES):
            tile = col0 // LANES + t
            yt = y[:, t * LANES:(t + 1) * LANES]
            if tile < n_q + n_k:
                is_q = tile < n_q
                if mixer == "b":
                    yt = yt * _rms_scale(yt) * (qn_ref[...] if is_q else kn_ref[...])
                yt = _rotate(yt, c, sa, sb, shift)
                if is_q:
                    yt = yt * q_scale
            elif tile >= n_q + n_k + n_plain:
                yt = _silu(yt)
            o_ref[rws, tile * LANES:(tile + 1) * LANES] = yt.astype(o_ref.dtype)


def _inproj(x2, seq, norm_w, w_in, tables, qn, kn, mixer):
    n_tok = x2.shape[0]
    n_out = w_in.shape[1]
    rows = FIRST_PROJ_ROWS
    per_seq = seq // rows
    tab_spec = pl.BlockSpec((rows, LANES), lambda i: (i % per_seq, 0))
    vec_spec = pl.BlockSpec((1, LANES), lambda i: (0, 0))
    return pl.pallas_call(
        functools.partial(_inproj_kernel, mixer=mixer),
        grid=(n_tok // rows,),
        in_specs=[
            pl.BlockSpec((rows, D_MODEL), lambda i: (i, 0)),
            pl.BlockSpec((1, D_MODEL), lambda i: (0, 0)),
            pl.BlockSpec((D_MODEL, n_out), lambda i: (0, 0)),
            tab_spec, tab_spec, tab_spec, vec_spec, vec_spec,
        ],
        out_specs=pl.BlockSpec((rows, n_out), lambda i: (i, 0)),
        out_shape=jax.ShapeDtypeStruct((n_tok, n_out), jnp.bfloat16),
        compiler_params=_compiler_params(("parallel",)),
        name=f"inproj_{mixer}",
    )(x2, norm_w.reshape(1, D_MODEL), w_in, *tables,
      qn.reshape(1, LANES), kn.reshape(1, LANES))


def _lane_fold(x, op):
    out = x[:, 0:LANES]
    for t in range(1, x.shape[1] // LANES):
        out = op(out, x[:, t * LANES:(t + 1) * LANES])
    return out


def _attention_groups(make_qs, finish, n_groups, k_ref, v_ref, vaug_scr, acc_scr, bufs):
    n_chunks, _, rows = bufs[0][0].shape
    for c in range(n_chunks):
        keys = slice(c * KEY_CHUNK, (c + 1) * KEY_CHUNK)
        vaug_scr[0:LANES, keys] = v_ref[keys, :].astype(jnp.float32).T.astype(vaug_scr.dtype)
    vaug_scr[LANES:V_ROWS, :] = jnp.ones((V_ROWS - LANES, v_ref.shape[0]), vaug_scr.dtype)

    def score_chunk(qs, c, s_scr, m_run):
        s = jnp.dot(k_ref[c * KEY_CHUNK:(c + 1) * KEY_CHUNK, :], qs,
                    preferred_element_type=jnp.float32)
        s_scr[c] = s
        m_c = jnp.max(s.reshape(KEY_CHUNK // SUBLANES, SUBLANES, rows), axis=0)
        return m_c if m_run is None else jnp.maximum(m_run, m_c)

    def weigh_chunk(c, s_scr, m_cur, acc):
        p = jnp.exp2(s_scr[c] - m_cur).astype(jnp.bfloat16)
        pv = jnp.dot(vaug_scr[:, c * KEY_CHUNK:(c + 1) * KEY_CHUNK], p,
                     preferred_element_type=jnp.float32)
        return pv if acc is None else acc + pv

    def stage(g_cur, g_next, parity, defer_finish=False):
        s_cur, m_ref = bufs[parity]
        s_next, m_next = bufs[1 - parity]
        if g_next is not None:
            qs_next = make_qs(g_next)
        if g_cur is not None:
            m_cur = m_ref[0:1, :]
        m_run, acc = None, None
        for c in range(n_chunks):
            if g_next is not None:
                m_run = score_chunk(qs_next, c, s_next, m_run)
            if g_cur is not None:
                acc = weigh_chunk(c, s_cur, m_cur, acc)
        if g_next is not None:
            m = jnp.max(m_run, axis=0, keepdims=True)
            m_next[...] = jnp.broadcast_to(m, (SUBLANES, rows))
        if g_cur is not None:
            if defer_finish:
                acc_scr[...] = acc
            else:
                finish(g_cur, acc)

    stage(None, 0, 1)
    stage(0, 1, 0, defer_finish=True)

    def stages(g, count):
        finish(g, acc_scr[...])
        for u in range(1, count + 1):
            stage(g + u, g + u + 1, u % 2, defer_finish=(u == count))

    unroll = GROUP_UNROLL if n_groups - 2 >= 2 * GROUP_UNROLL else 2

    def body(j, carry):
        stages(unroll * j, unroll)
        return carry

    n_loop = (n_groups - 2) // unroll
    lax.fori_loop(0, n_loop, body, 0)
    done = unroll * n_loop
    if n_groups - 2 > done:
        stages(done, n_groups - 2 - done)
    finish(n_groups - 2, acc_scr[...])
    stage(n_groups - 1, None, 1)


def _transpose_bf16(x):
    return x.astype(jnp.float32).T.astype(jnp.bfloat16)


def _row_start(g, bq):
    return g * bq if isinstance(g, int) else pl.multiple_of(g * bq, bq)


def _attn_a_kernel(lam_ref, subln_ref, q_ref, k_ref, v_ref, g_ref, o_ref,
                   vaug_scr, acc_scr, s0_scr, m0_scr, s1_scr, m1_scr, *, lam_init):
    bq = ATTN_ROWS // 2
    n_groups = q_ref.shape[0] // bq
    lam = lam_ref[...]
    lam_full = (jnp.exp(jnp.sum(lam[0:1] * lam[1:2], axis=-1, keepdims=True))
                - jnp.exp(jnp.sum(lam[2:3] * lam[3:4], axis=-1, keepdims=True))
                + lam_init)
    subln = subln_ref[...] * (1.0 - lam_init)

    def make_qs(g):
        q = q_ref[pl.ds(_row_start(g, bq), bq), :]
        lane = lax.broadcasted_iota(jnp.int32, q.shape, 1)
        zero = jnp.zeros_like(q)
        qs = jnp.concatenate([jnp.where(lane < A_QK_DIM, q, zero),
                              jnp.where(lane >= A_QK_DIM, q, zero)], axis=0)
        return _transpose_bf16(qs)

    def finish(g, acc):
        r0 = _row_start(g, bq)
        o = acc[0:LANES, :] / acc[LANES:LANES + 1, :]
        o = (o[:, :bq] - lam_full * o[:, bq:]).T
        o = o * _rms_scale(o) * subln
        gate = g_ref[pl.ds(r0, bq), :].astype(jnp.float32)
        o_ref[pl.ds(r0, bq), :] = (o * gate).astype(o_ref.dtype)

    _attention_groups(make_qs, finish, n_groups, k_ref, v_ref, vaug_scr, acc_scr,
                      ((s0_scr, m0_scr), (s1_scr, m1_scr)))


def _attn_scratch(seq):
    n_chunks = seq // KEY_CHUNK
    scores = pltpu.VMEM((n_chunks, KEY_CHUNK, ATTN_ROWS), jnp.float32)
    maxima = pltpu.VMEM((SUBLANES, ATTN_ROWS), jnp.float32)
    return [pltpu.VMEM((V_ROWS, seq), jnp.bfloat16),
            pltpu.VMEM((V_ROWS, ATTN_ROWS), jnp.float32),
            scores, maxima, scores, maxima]


def _attn_a(proj, batch, seq, lam, subln, lam_init):
    head_spec = lambda tile0: pl.BlockSpec((seq, LANES), lambda b, h: (b, tile0 + h))
    return pl.pallas_call(
        functools.partial(_attn_a_kernel, lam_init=lam_init),
        grid=(batch, A_HEADS),
        in_specs=[
            pl.BlockSpec((4, A_QK_DIM), lambda b, h: (0, 0)),
            pl.BlockSpec((1, LANES), lambda b, h: (0, 0)),
            head_spec(0), head_spec(A_HEADS), head_spec(2 * A_HEADS), head_spec(3 * A_HEADS),
        ],
        out_specs=head_spec(0),
        out_shape=jax.ShapeDtypeStruct((batch * seq, A_HEADS * LANES), jnp.bfloat16),
        scratch_shapes=_attn_scratch(seq),
        compiler_params=_compiler_params(("parallel", "parallel")),
        name="attn_a",
    )(lam, subln.reshape(1, LANES), proj, proj, proj, proj)


def _attn_b_kernel(q_ref, k_ref, v_ref, g_ref, o_ref,
                   vaug_scr, acc_scr, s0_scr, m0_scr, s1_scr, m1_scr):
    bq = ATTN_ROWS // B_GROUP
    n_groups = q_ref.shape[0] // bq

    def make_qs(g):
        r0 = _row_start(g, bq)
        return _transpose_bf16(jnp.concatenate(
            [q_ref[pl.ds(r0, bq), i * LANES:(i + 1) * LANES] for i in range(B_GROUP)], axis=0))

    def finish(g, acc):
        r0 = _row_start(g, bq)
        o = (acc[0:LANES, :] / acc[LANES:LANES + 1, :]).T
        for i in range(B_GROUP):
            gate = g_ref[pl.ds(r0, bq), i * LANES:(i + 1) * LANES].astype(jnp.float32)
            o_ref[pl.ds(r0, bq), i * LANES:(i + 1) * LANES] = (
                o[i * bq:(i + 1) * bq] * gate).astype(o_ref.dtype)

    _attention_groups(make_qs, finish, n_groups, k_ref, v_ref, vaug_scr, acc_scr,
                      ((s0_scr, m0_scr), (s1_scr, m1_scr)))


def _attn_b(proj, batch, seq):
    q_rows = min(seq, B_SPLIT_ROWS)
    n_split = seq // q_rows
    group_cols = B_GROUP * LANES
    k_tile0 = B_HEADS
    v_tile0 = B_HEADS + B_KV_HEADS
    g_blk0 = (B_HEADS + 2 * B_KV_HEADS) * LANES // group_cols
    q_spec = lambda blk0: pl.BlockSpec(
        (q_rows, group_cols), lambda b, kv, i: (b * n_split + i, blk0 + kv))
    kv_spec = lambda tile0: pl.BlockSpec((seq, LANES), lambda b, kv, i: (b, tile0 + kv))
    return pl.pallas_call(
        _attn_b_kernel,
        grid=(batch, B_KV_HEADS, n_split),
        in_specs=[q_spec(0), kv_spec(k_tile0), kv_spec(v_tile0), q_spec(g_blk0)],
        out_specs=q_spec(0),
        out_shape=jax.ShapeDtypeStruct((batch * seq, B_HEADS * LANES), jnp.bfloat16),
        scratch_shapes=_attn_scratch(seq),
        compiler_params=_compiler_params(("parallel", "parallel", "parallel")),
        name="attn_b",
    )(proj, proj, proj, proj)


def _outproj_rows(o_ref, w_ref, nw_ref, x_ref):
    m = jnp.dot(o_ref[...], w_ref[...], preferred_element_type=jnp.float32)
    return x_ref[...] + m * _rms_scale(m) * nw_ref[...]


def _outproj_kernel(o_ref, w_ref, nw_ref, x_ref, y_ref):
    y_ref[...] = _outproj_rows(o_ref, w_ref, nw_ref, x_ref)


def _out_in_kernel(o_ref, wo_ref, nwo_ref, x_ref, nwi_ref, wi_ref, c_ref, sa_ref, sb_ref,
                   qn_ref, kn_ref, y_ref, p_ref, h_scr, *, mixer):
    @pl.when(pl.program_id(0) == 0)
    def _():
        h_scr[...] = jnp.zeros_like(h_scr)

    h_prev = h_scr[...]
    y = _outproj_rows(o_ref, wo_ref, nwo_ref, x_ref)
    y_ref[...] = y
    _inproj_rows(h_prev, wi_ref, c_ref, sa_ref, sb_ref, qn_ref, kn_ref, p_ref, mixer)
    h_scr[...] = _pre_norm(y, nwi_ref)


def _out_in(o, w_out, norm_post, x2, seq, norm_pre, w_in, tables, qn, kn, mixer):
    n_tok = x2.shape[0]
    n_out = w_in.shape[1]
    rows = PROJ_ROWS[mixer]
    n_tiles = n_tok // rows
    per_seq = seq // rows
    cur = lambda i: (jnp.minimum(i, n_tiles - 1), 0)
    prev = lambda i: (jnp.maximum(i - 1, 0), 0)
    const = lambda shape: pl.BlockSpec(shape, lambda i: (0, 0))
    tab_spec = pl.BlockSpec((rows, LANES), lambda i: (jnp.maximum(i - 1, 0) % per_seq, 0))
    return pl.pallas_call(
        functools.partial(_out_in_kernel, mixer=mixer),
        grid=(n_tiles + 1,),
        in_specs=[
            pl.BlockSpec((rows, D_MODEL), cur), const((D_MODEL, D_MODEL)),
            const((1, D_MODEL)), pl.BlockSpec((rows, D_MODEL), cur),
            const((1, D_MODEL)), const((D_MODEL, n_out)),
            tab_spec, tab_spec, tab_spec, const((1, LANES)), const((1, LANES)),
        ],
        out_specs=[pl.BlockSpec((rows, D_MODEL), cur), pl.BlockSpec((rows, n_out), prev)],
        out_shape=[jax.ShapeDtypeStruct((n_tok, D_MODEL), jnp.float32),
                   jax.ShapeDtypeStruct((n_tok, n_out), jnp.bfloat16)],
        scratch_shapes=[pltpu.VMEM((rows, D_MODEL), jnp.bfloat16)],
        compiler_params=_compiler_params(("arbitrary",)),
        name=f"out_in_{mixer}",
    )(o, w_out, norm_post.reshape(1, D_MODEL), x2, norm_pre.reshape(1, D_MODEL), w_in,
      *tables, qn.reshape(1, LANES), kn.reshape(1, LANES))


def _outproj(o, w_out, norm_w, x2):
    n_tok = x2.shape[0]
    row_spec = pl.BlockSpec((OUT_ROWS, D_MODEL), lambda i: (i, 0))
    return pl.pallas_call(
        _outproj_kernel,
        grid=(n_tok // OUT_ROWS,),
        in_specs=[
            row_spec,
            pl.BlockSpec((D_MODEL, D_MODEL), lambda i: (0, 0)),
            pl.BlockSpec((1, D_MODEL), lambda i: (0, 0)),
            row_spec,
        ],
        out_specs=row_spec,
        out_shape=jax.ShapeDtypeStruct((n_tok, D_MODEL), jnp.float32),
        compiler_params=_compiler_params(("parallel",)),
        name="outproj",
    )(o, w_out, norm_w.reshape(1, D_MODEL), x2)


def _trunk(x, norm_pre, norm_post, a_w_in, a_w_out, a_lam, a_subln,
           b_w_in, b_w_out, b_q_norm, b_k_norm):
    batch, seq, _ = x.shape
    x2 = x.reshape(batch * seq, D_MODEL)
    tables_a = _rope_tables_a(seq)
    tables_b = _rope_tables_b(seq)
    ones = jnp.ones((LANES,), jnp.float32)

    def inproj_args(i):
        j = i // 2
        if i % 2 == 0:
            return (seq, norm_pre[i], a_w_in[j], tables_a, ones, ones, "a")
        return (seq, norm_pre[i], b_w_in[j], tables_b, b_q_norm[j], b_k_norm[j], "b")

    proj = _inproj(x2, *inproj_args(0))
    for i in range(DEPTH):
        j = i // 2
        if i % 2 == 0:
            lam_init = 0.8 - 0.6 * math.exp(-0.3 * i)
            o = _attn_a(proj, batch, seq, a_lam[j], a_subln[j], lam_init)
            w_out = a_w_out[j]
        else:
            o = _attn_b(proj, batch, seq)
            w_out = b_w_out[j]
        if i + 1 < DEPTH:
            x2, proj = _out_in(o, w_out, norm_post[i], x2, *inproj_args(i + 1))
        else:
            x2 = _outproj(o, w_out, norm_post[i], x2)
    return x2.reshape(batch, seq, D_MODEL)


def kernel(x_prompt, x_sample, norm_pre, norm_post, a_w_in, a_w_out, a_lam, a_subln,
           b_w_in, b_w_out, b_q_norm, b_k_norm):
    bf16 = jnp.bfloat16
    weights = (norm_pre, norm_post, a_w_in.astype(bf16), a_w_out.astype(bf16), a_lam,
               a_subln, b_w_in.astype(bf16), b_w_out.astype(bf16), b_q_norm, b_k_norm)
    return (_trunk(x_prompt, *weights), _trunk(x_sample, *weights))
```

```python
import functools
import math

import jax
import jax.numpy as jnp
from jax import lax
from jax.experimental import pallas as pl
from jax.experimental.pallas import tpu as pltpu

D_MODEL = 1024
DEPTH = 4
NORM_EPS = 1e-6

A_HEADS = 8
A_QK_DIM = 64
A_ROT_DIM = 16
ROPE_THETA = 500000.0

B_HEADS = 8
B_KV_HEADS = 2
B_GROUP = B_HEADS // B_KV_HEADS
B_HEAD_DIM = 128
B_AXIS_DIM = 64
AXIAL_THETA = 10000.0
GRID_W = 64

LANES = 128
SUBLANES = 8
V_ROWS = LANES + 16
LOG2E = 1.4426950408889634
VMEM_LIMIT_BYTES = 56 * 1024 * 1024

PROJ_ROWS = {"a": 512, "b": 512}
PROJ_SUB_ROWS = {"a": 512, "b": 256}
FIRST_PROJ_ROWS = 1024
OUT_ROWS = 1024
PROJ_COLS = 512
ATTN_ROWS = 512
KEY_CHUNK = 512
GROUP_UNROLL = 6
B_SPLIT_ROWS = 4096


def _compiler_params(semantics):
    return pltpu.CompilerParams(
        dimension_semantics=semantics, vmem_limit_bytes=VMEM_LIMIT_BYTES)


def _angles(pos, dim, theta):
    inv_freq = theta ** (-(jnp.arange(0, dim, 2, dtype=jnp.float32) / dim))
    return pos[:, None] * inv_freq[None, :]


def _rope_tables_a(seq):
    ang = _angles(jnp.arange(seq, dtype=jnp.float32), A_ROT_DIM, ROPE_THETA)
    cos, sin = jnp.cos(ang), jnp.sin(ang)
    half = A_ROT_DIM // 2
    one = jnp.ones((seq, A_QK_DIM - A_ROT_DIM), jnp.float32)
    zero_h = jnp.zeros((seq, half), jnp.float32)
    zero_r = jnp.zeros((seq, A_QK_DIM - A_ROT_DIM), jnp.float32)
    c = jnp.concatenate([cos, cos, one], axis=1)
    sa = jnp.concatenate([-sin, zero_h, zero_r], axis=1)
    sb = jnp.concatenate([zero_h, sin, zero_r], axis=1)
    tile2 = lambda t: jnp.concatenate([t, t], axis=1)
    return tile2(c), tile2(sa), tile2(sb)


def _rope_tables_b(seq):
    rows = seq // GRID_W
    row_ids = jnp.repeat(jnp.arange(rows, dtype=jnp.float32), GRID_W)
    col_ids = jnp.tile(jnp.arange(GRID_W, dtype=jnp.float32), rows)
    ar = _angles(row_ids, B_AXIS_DIM, AXIAL_THETA)
    ac = _angles(col_ids, B_AXIS_DIM, AXIAL_THETA)
    z = jnp.zeros_like(ar)
    c = jnp.concatenate([jnp.cos(ar), jnp.cos(ar), jnp.cos(ac), jnp.cos(ac)], axis=1)
    sa = jnp.concatenate([-jnp.sin(ar), z, -jnp.sin(ac), z], axis=1)
    sb = jnp.concatenate([z, jnp.sin(ar), z, jnp.sin(ac)], axis=1)
    return c, sa, sb


def _rotate(y, c, sa, sb, shift):
    return (y * c + pltpu.roll(y, LANES - shift, axis=1) * sa
            + pltpu.roll(y, shift, axis=1) * sb)


def _silu(g):
    return g / (1.0 + jnp.exp(-g))


def _rms_scale(y):
    return lax.rsqrt(jnp.mean(y * y, axis=-1, keepdims=True) + NORM_EPS)


def _pre_norm(x, nw_ref):
    return (x * _rms_scale(x) * nw_ref[...]).astype(jnp.bfloat16)


def _inproj_kernel(x_ref, nw_ref, w_ref, c_ref, sa_ref, sb_ref, qn_ref, kn_ref,
                   o_ref, *, mixer):
    _inproj_rows(_pre_norm(x_ref[...], nw_ref), w_ref, c_ref, sa_ref, sb_ref,
                 qn_ref, kn_ref, o_ref, mixer)


def _inproj_rows(h, w_ref, c_ref, sa_ref, sb_ref, qn_ref, kn_ref, o_ref, mixer):
    sub = PROJ_SUB_ROWS[mixer]
    for r0 in range(0, h.shape[0], sub):
        rws = slice(r0, r0 + sub)
        _inproj_sub_rows(h[rws], w_ref, c_ref[rws, :], sa_ref[rws, :], sb_ref[rws, :],
                         qn_ref, kn_ref, o_ref, rws, mixer)


def _inproj_sub_rows(h, w_ref, c, sa, sb, qn_ref, kn_ref, o_ref, rws, mixer):
    n_out = o_ref.shape[1]
    if mixer == "a":
        n_q, n_k, shift = A_HEADS, A_HEADS, A_ROT_DIM // 2
        q_scale = (A_QK_DIM ** -0.5) * LOG2E
        n_plain = A_HEADS
    else:
        n_q, n_k, shift = B_HEADS, B_KV_HEADS, B_AXIS_DIM // 2
        q_scale = (B_HEAD_DIM ** -0.5) * LOG2E
        n_plain = B_KV_HEADS
    for col0 in range(0, n_out, PROJ_COLS):
        y = jnp.dot(h, w_ref[:, col0:col0 + PROJ_COLS],
                    preferred_element_type=jnp.float32)
        for t in range(PROJ_COLS // LANES):
            tile = col0 // LANES + t
            yt = y[:, t * LANES:(t + 1) * LANES]
            if tile < n_q + n_k:
                is_q = tile < n_q
                if mixer == "b":
                    yt = yt * _rms_scale(yt) * (qn_ref[...] if is_q else kn_ref[...])
                yt = _rotate(yt, c, sa, sb, shift)
                if is_q:
                    yt = yt * q_scale
            elif tile >= n_q + n_k + n_plain:
                yt = _silu(yt)
            o_ref[rws, tile * LANES:(tile + 1) * LANES] = yt.astype(o_ref.dtype)


def _inproj(x2, seq, norm_w, w_in, tables, qn, kn, mixer):
    n_tok = x2.shape[0]
    n_out = w_in.shape[1]
    rows = FIRST_PROJ_ROWS
    per_seq = seq // rows
    tab_spec = pl.BlockSpec((rows, LANES), lambda i: (i % per_seq, 0))
    vec_spec = pl.BlockSpec((1, LANES), lambda i: (0, 0))
    return pl.pallas_call(
        functools.partial(_inproj_kernel, mixer=mixer),
        grid=(n_tok // rows,),
        in_specs=[
            pl.BlockSpec((rows, D_MODEL), lambda i: (i, 0)),
            pl.BlockSpec((1, D_MODEL), lambda i: (0, 0)),
            pl.BlockSpec((D_MODEL, n_out), lambda i: (0, 0)),
            tab_spec, tab_spec, tab_spec, vec_spec, vec_spec,
        ],
        out_specs=pl.BlockSpec((rows, n_out), lambda i: (i, 0)),
        out_shape=jax.ShapeDtypeStruct((n_tok, n_out), jnp.bfloat16),
        compiler_params=_compiler_params(("parallel",)),
        name=f"inproj_{mixer}",
    )(x2, norm_w.reshape(1, D_MODEL), w_in, *tables,
      qn.reshape(1, LANES), kn.reshape(1, LANES))


def _lane_fold(x, op):
    out = x[:, 0:LANES]
    for t in range(1, x.shape[1] // LANES):
        out = op(out, x[:, t * LANES:(t + 1) * LANES])
    return out


def _attention_groups(make_qs, finish, n_groups, k_ref, v_ref, vaug_scr, acc_scr, bufs):
    n_chunks, _, rows = bufs[0][0].shape
    for c in range(n_chunks):
        keys = slice(c * KEY_CHUNK, (c + 1) * KEY_CHUNK)
        vaug_scr[0:LANES, keys] = v_ref[keys, :].T
    vaug_scr[LANES:V_ROWS, :] = jnp.ones((V_ROWS - LANES, v_ref.shape[0]), vaug_scr.dtype)

    def score_chunk(qs, c, s_scr, m_run):
        s = jnp.dot(k_ref[c * KEY_CHUNK:(c + 1) * KEY_CHUNK, :], qs,
                    preferred_element_type=jnp.float32)
        s_scr[c] = s
        m_c = jnp.max(s.reshape(KEY_CHUNK // SUBLANES, SUBLANES, rows), axis=0)
        return m_c if m_run is None else jnp.maximum(m_run, m_c)

    def weigh_chunk(c, s_scr, m_cur, acc):
        p = jnp.exp2(s_scr[c] - m_cur).astype(jnp.bfloat16)
        pv = jnp.dot(vaug_scr[:, c * KEY_CHUNK:(c + 1) * KEY_CHUNK], p,
                     preferred_element_type=jnp.float32)
        return pv if acc is None else acc + pv

    def stage(g_cur, g_next, parity, defer_finish=False):
        s_cur, m_ref = bufs[parity]
        s_next, m_next = bufs[1 - parity]
        if g_next is not None:
            qs_next = make_qs(g_next)
        if g_cur is not None:
            m_cur = m_ref[0:1, :]
        m_run, acc = None, None
        for c in range(n_chunks):
            if g_next is not None:
                m_run = score_chunk(qs_next, c, s_next, m_run)
            if g_cur is not None:
                acc = weigh_chunk(c, s_cur, m_cur, acc)
        if g_next is not None:
            m = jnp.max(m_run, axis=0, keepdims=True)
            m_next[...] = jnp.broadcast_to(m, (SUBLANES, rows))
        if g_cur is not None:
            if defer_finish:
                acc_scr[...] = acc
            else:
                finish(g_cur, acc)

    stage(None, 0, 1)
    stage(0, 1, 0, defer_finish=True)

    def stages(g, count):
        finish(g, acc_scr[...])
        for u in range(1, count + 1):
            stage(g + u, g + u + 1, u % 2, defer_finish=(u == count))

    unroll = GROUP_UNROLL if n_groups - 2 >= 2 * GROUP_UNROLL else 2

    def body(j, carry):
        stages(unroll * j, unroll)
        return carry

    n_loop = (n_groups - 2) // unroll
    lax.fori_loop(0, n_loop, body, 0)
    done = unroll * n_loop
    if n_groups - 2 > done:
        stages(done, n_groups - 2 - done)
    finish(n_groups - 2, acc_scr[...])
    stage(n_groups - 1, None, 1)


def _transpose_bf16(x):
    return x.T


def _row_start(g, bq):
    return g * bq if isinstance(g, int) else pl.multiple_of(g * bq, bq)


def _attn_a_kernel(lam_ref, subln_ref, q_ref, k_ref, v_ref, g_ref, o_ref,
                   vaug_scr, acc_scr, s0_scr, m0_scr, s1_scr, m1_scr, *, lam_init):
    bq = ATTN_ROWS // 2
    n_groups = q_ref.shape[0] // bq
    lam = lam_ref[...]
    lam_full = (jnp.exp(jnp.sum(lam[0:1] * lam[1:2], axis=-1, keepdims=True))
                - jnp.exp(jnp.sum(lam[2:3] * lam[3:4], axis=-1, keepdims=True))
                + lam_init)
    subln = subln_ref[...] * (1.0 - lam_init)

    def make_qs(g):
        q = q_ref[pl.ds(_row_start(g, bq), bq), :]
        lane = lax.broadcasted_iota(jnp.int32, q.shape, 1)
        zero = jnp.zeros_like(q)
        qs = jnp.concatenate([jnp.where(lane < A_QK_DIM, q, zero),
                              jnp.where(lane >= A_QK_DIM, q, zero)], axis=0)
        return _transpose_bf16(qs)

    def finish(g, acc):
        r0 = _row_start(g, bq)
        o = acc[0:LANES, :] / acc[LANES:LANES + 1, :]
        o = (o[:, :bq] - lam_full * o[:, bq:]).T
        o = o * _rms_scale(o) * subln
        gate = g_ref[pl.ds(r0, bq), :].astype(jnp.float32)
        o_ref[pl.ds(r0, bq), :] = (o * gate).astype(o_ref.dtype)

    _attention_groups(make_qs, finish, n_groups, k_ref, v_ref, vaug_scr, acc_scr,
                      ((s0_scr, m0_scr), (s1_scr, m1_scr)))


def _attn_scratch(seq):
    n_chunks = seq // KEY_CHUNK
    scores = pltpu.VMEM((n_chunks, KEY_CHUNK, ATTN_ROWS), jnp.float32)
    maxima = pltpu.VMEM((SUBLANES, ATTN_ROWS), jnp.float32)
    return [pltpu.VMEM((V_ROWS, seq), jnp.bfloat16),
            pltpu.VMEM((V_ROWS, ATTN_ROWS), jnp.float32),
            scores, maxima, scores, maxima]


def _attn_a(proj, batch, seq, lam, subln, lam_init):
    head_spec = lambda tile0: pl.BlockSpec((seq, LANES), lambda b, h: (b, tile0 + h))
    return pl.pallas_call(
        functools.partial(_attn_a_kernel, lam_init=lam_init),
        grid=(batch, A_HEADS),
        in_specs=[
            pl.BlockSpec((4, A_QK_DIM), lambda b, h: (0, 0)),
            pl.BlockSpec((1, LANES), lambda b, h: (0, 0)),
            head_spec(0), head_spec(A_HEADS), head_spec(2 * A_HEADS), head_spec(3 * A_HEADS),
        ],
        out_specs=head_spec(0),
        out_shape=jax.ShapeDtypeStruct((batch * seq, A_HEADS * LANES), jnp.bfloat16),
        scratch_shapes=_attn_scratch(seq),
        compiler_params=_compiler_params(("parallel", "parallel")),
        name="attn_a",
    )(lam, subln.reshape(1, LANES), proj, proj, proj, proj)


def _attn_b_kernel(q_ref, k_ref, v_ref, g_ref, o_ref,
                   vaug_scr, acc_scr, s0_scr, m0_scr, s1_scr, m1_scr):
    bq = ATTN_ROWS // B_GROUP
    n_groups = q_ref.shape[0] // bq

    def make_qs(g):
        r0 = _row_start(g, bq)
        return _transpose_bf16(jnp.concatenate(
            [q_ref[pl.ds(r0, bq), i * LANES:(i + 1) * LANES] for i in range(B_GROUP)], axis=0))

    def finish(g, acc):
        r0 = _row_start(g, bq)
        o = (acc[0:LANES, :] / acc[LANES:LANES + 1, :]).T
        for i in range(B_GROUP):
            gate = g_ref[pl.ds(r0, bq), i * LANES:(i + 1) * LANES].astype(jnp.float32)
            o_ref[pl.ds(r0, bq), i * LANES:(i + 1) * LANES] = (
                o[i * bq:(i + 1) * bq] * gate).astype(o_ref.dtype)

    _attention_groups(make_qs, finish, n_groups, k_ref, v_ref, vaug_scr, acc_scr,
                      ((s0_scr, m0_scr), (s1_scr, m1_scr)))


def _attn_b(proj, batch, seq):
    q_rows = min(seq, B_SPLIT_ROWS)
    n_split = seq // q_rows
    group_cols = B_GROUP * LANES
    k_tile0 = B_HEADS
    v_tile0 = B_HEADS + B_KV_HEADS
    g_blk0 = (B_HEADS + 2 * B_KV_HEADS) * LANES // group_cols
    q_spec = lambda blk0: pl.BlockSpec(
        (q_rows, group_cols), lambda b, kv, i: (b * n_split + i, blk0 + kv))
    kv_spec = lambda tile0: pl.BlockSpec((seq, LANES), lambda b, kv, i: (b, tile0 + kv))
    return pl.pallas_call(
        _attn_b_kernel,
        grid=(batch, B_KV_HEADS, n_split),
        in_specs=[q_spec(0), kv_spec(k_tile0), kv_spec(v_tile0), q_spec(g_blk0)],
        out_specs=q_spec(0),
        out_shape=jax.ShapeDtypeStruct((batch * seq, B_HEADS * LANES), jnp.bfloat16),
        scratch_shapes=_attn_scratch(seq),
        compiler_params=_compiler_params(("parallel", "parallel", "parallel")),
        name="attn_b",
    )(proj, proj, proj, proj)


def _outproj_rows(o_ref, w_ref, nw_ref, x_ref):
    m = jnp.dot(o_ref[...], w_ref[...], preferred_element_type=jnp.float32)
    return x_ref[...] + m * _rms_scale(m) * nw_ref[...]


def _outproj_kernel(o_ref, w_ref, nw_ref, x_ref, y_ref):
    y_ref[...] = _outproj_rows(o_ref, w_ref, nw_ref, x_ref)


def _out_in_kernel(o_ref, wo_ref, nwo_ref, x_ref, nwi_ref, wi_ref, c_ref, sa_ref, sb_ref,
                   qn_ref, kn_ref, y_ref, p_ref, h_scr, *, mixer):
    @pl.when(pl.program_id(0) == 0)
    def _():
        h_scr[...] = jnp.zeros_like(h_scr)

    h_prev = h_scr[...]
    y = _outproj_rows(o_ref, wo_ref, nwo_ref, x_ref)
    y_ref[...] = y
    _inproj_rows(h_prev, wi_ref, c_ref, sa_ref, sb_ref, qn_ref, kn_ref, p_ref, mixer)
    h_scr[...] = _pre_norm(y, nwi_ref)


def _out_in(o, w_out, norm_post, x2, seq, norm_pre, w_in, tables, qn, kn, mixer):
    n_tok = x2.shape[0]
    n_out = w_in.shape[1]
    rows = PROJ_ROWS[mixer]
    n_tiles = n_tok // rows
    per_seq = seq // rows
    cur = lambda i: (jnp.minimum(i, n_tiles - 1), 0)
    prev = lambda i: (jnp.maximum(i - 1, 0), 0)
    const = lambda shape: pl.BlockSpec(shape, lambda i: (0, 0))
    tab_spec = pl.BlockSpec((rows, LANES), lambda i: (jnp.maximum(i - 1, 0) % per_seq, 0))
    return pl.pallas_call(
        functools.partial(_out_in_kernel, mixer=mixer),
        grid=(n_tiles + 1,),
        in_specs=[
            pl.BlockSpec((rows, D_MODEL), cur), const((D_MODEL, D_MODEL)),
            const((1, D_MODEL)), pl.BlockSpec((rows, D_MODEL), cur),
            const((1, D_MODEL)), const((D_MODEL, n_out)),
            tab_spec, tab_spec, tab_spec, const((1, LANES)), const((1, LANES)),
        ],
        out_specs=[pl.BlockSpec((rows, D_MODEL), cur), pl.BlockSpec((rows, n_out), prev)],
        out_shape=[jax.ShapeDtypeStruct((n_tok, D_MODEL), jnp.float32),
                   jax.ShapeDtypeStruct((n_tok, n_out), jnp.bfloat16)],
        scratch_shapes=[pltpu.VMEM((rows, D_MODEL), jnp.bfloat16)],
        compiler_params=_compiler_params(("arbitrary",)),
        name=f"out_in_{mixer}",
    )(o, w_out, norm_post.reshape(1, D_MODEL), x2, norm_pre.reshape(1, D_MODEL), w_in,
      *tables, qn.reshape(1, LANES), kn.reshape(1, LANES))


def _outproj(o, w_out, norm_w, x2):
    n_tok = x2.shape[0]
    row_spec = pl.BlockSpec((OUT_ROWS, D_MODEL), lambda i: (i, 0))
    return pl.pallas_call(
        _outproj_kernel,
        grid=(n_tok // OUT_ROWS,),
        in_specs=[
            row_spec,
            pl.BlockSpec((D_MODEL, D_MODEL), lambda i: (0, 0)),
            pl.BlockSpec((1, D_MODEL), lambda i: (0, 0)),
            row_spec,
        ],
        out_specs=row_spec,
        out_shape=jax.ShapeDtypeStruct((n_tok, D_MODEL), jnp.float32),
        compiler_params=_compiler_params(("parallel",)),
        name="outproj",
    )(o, w_out, norm_w.reshape(1, D_MODEL), x2)


def _trunk(x, norm_pre, norm_post, a_w_in, a_w_out, a_lam, a_subln,
           b_w_in, b_w_out, b_q_norm, b_k_norm):
    batch, seq, _ = x.shape
    x2 = x.reshape(batch * seq, D_MODEL)
    tables_a = _rope_tables_a(seq)
    tables_b = _rope_tables_b(seq)
    ones = jnp.ones((LANES,), jnp.float32)

    def inproj_args(i):
        j = i // 2
        if i % 2 == 0:
            return (seq, norm_pre[i], a_w_in[j], tables_a, ones, ones, "a")
        return (seq, norm_pre[i], b_w_in[j], tables_b, b_q_norm[j], b_k_norm[j], "b")

    proj = _inproj(x2, *inproj_args(0))
    for i in range(DEPTH):
        j = i // 2
        if i % 2 == 0:
            lam_init = 0.8 - 0.6 * math.exp(-0.3 * i)
            o = _attn_a(proj, batch, seq, a_lam[j], a_subln[j], lam_init)
            w_out = a_w_out[j]
        else:
            o = _attn_b(proj, batch, seq)
            w_out = b_w_out[j]
        if i + 1 < DEPTH:
            x2, proj = _out_in(o, w_out, norm_post[i], x2, *inproj_args(i + 1))
        else:
            x2 = _outproj(o, w_out, norm_post[i], x2)
    return x2.reshape(batch, seq, D_MODEL)


def kernel(x_prompt, x_sample, norm_pre, norm_post, a_w_in, a_w_out, a_lam, a_subln,
           b_w_in, b_w_out, b_q_norm, b_k_norm):
    bf16 = jnp.bfloat16
    weights = (norm_pre, norm_post, a_w_in.astype(bf16), a_w_out.astype(bf16), a_lam,
               a_subln, b_w_in.astype(bf16), b_w_out.astype(bf16), b_q_norm, b_k_norm)
    return (_trunk(x_prompt, *weights), _trunk(x_sample, *weights))
```
